```python
import math
import jax, jax.numpy as jnp
from jax import lax
import numpy as np

D_MODEL = 1024
BATCH = 32
SEQ = 2048
DEPTH = 4
DEC_BATCH = 8
DEC_SEQ = 8192
PAST_LEN = 128

A_HEADS = 8
A_HEAD_DIM = 64
A_WIDTH = A_HEADS * A_HEAD_DIM
A_CONV = 5
A_CHUNK = 64
B_GROUPS = ((128, 1), (512, 4), (2048, 16))
B_HEADS_PER_GROUP = 4
B_HEADS = B_HEADS_PER_GROUP * len(B_GROUPS)
B_HEAD_DIM = 64
B_WIDTH = B_HEADS * B_HEAD_DIM
B_OUT_WIDTH = B_HEADS_PER_GROUP * B_HEAD_DIM
B_BLOCK = 64
N_EXPERTS = 16
D_EXPERT = 1024
CAPACITY_FACTOR = 2
EPS = 1e-6
NEG = -1e30
SPLIT_POINTS = (3 * A_WIDTH,
                4 * A_WIDTH,
                4 * A_WIDTH + 2 * A_HEADS,
                4 * A_WIDTH + 4 * A_HEADS,
                4 * A_WIDTH + 4 * A_HEADS + 3 * B_WIDTH)
IN_WIDTH = 4 * A_WIDTH + 4 * A_HEADS + 3 * B_WIDTH + 2 * D_MODEL

kernel_name = "hybrid_deltanet_dilated_attn_ec_moe_encoder"


def rmsnorm(x, g):
    xf = x.astype(jnp.float32)
    y = xf * lax.rsqrt(jnp.mean(xf * xf, axis=-1, keepdims=True) + EPS)
    return (y * g.astype(jnp.float32)).astype(x.dtype)


def l2norm(x):
    xf = x.astype(jnp.float32)
    return xf * lax.rsqrt(jnp.sum(xf * xf, axis=-1, keepdims=True) + EPS)


def alibi_slopes():
    return jnp.asarray(2.0 ** (-8.0 * np.arange(1, B_HEADS + 1) / B_HEADS), dtype=jnp.float32)


def centred_depthwise_conv(x, w):
    c = x.shape[-1]
    pad = (A_CONV - 1) // 2
    return lax.conv_general_dilated(x, w.astype(x.dtype)[:, None, :], window_strides=(1,),
                                    padding=[(pad, pad)], dimension_numbers=('NWC', 'WIO', 'NWC'),
                                    feature_group_count=c)


def gated_delta_chunked(q, k, v, g, beta):
    bn, s, h, dk = q.shape
    dv = v.shape[-1]
    n = s // A_CHUNK
    f32 = jnp.float32

    def to_chunks(t):
        return t.astype(f32).reshape(bn, n, A_CHUNK, h, -1).transpose(1, 0, 3, 2, 4)

    qc = to_chunks(q) * (dk ** -0.5)
    kc, vc = to_chunks(k), to_chunks(v)
    gc = g.astype(f32).reshape(bn, n, A_CHUNK, h).transpose(1, 0, 3, 2)
    bc = beta.astype(f32).reshape(bn, n, A_CHUNK, h).transpose(1, 0, 3, 2)
    gcum = jnp.cumsum(gc, axis=-1)
    idx = jnp.arange(A_CHUNK)
    causal = idx[:, None] >= idx[None, :]
    strict = idx[:, None] > idx[None, :]
    decay = jnp.where(causal, jnp.exp(jnp.minimum(gcum[..., :, None] - gcum[..., None, :], 0.0)), 0.0)
    kb = kc * bc[..., None]
    lower = jnp.where(strict, jnp.einsum('nbhcd,nbhed->nbhce', kb, kc) * decay, 0.0)
    tmat = lower + jnp.eye(A_CHUNK, dtype=f32)
    rhs = jnp.concatenate([vc * bc[..., None], kb * jnp.exp(gcum)[..., None]], axis=-1)
    sol = lax.linalg.triangular_solve(tmat, rhs, left_side=True, lower=True, unit_diagonal=True)
    u, w = sol[..., :dv], sol[..., dv:]
    attn_intra = jnp.where(causal, jnp.einsum('nbhcd,nbhed->nbhce', qc, kc) * decay, 0.0)

    def step(state, inp):
        q_i, k_i, u_i, w_i, a_i, g_i = inp
        v_new = u_i - jnp.einsum('bhcd,bhdv->bhcv', w_i, state)
        o = (jnp.einsum('bhcd,bhdv->bhcv', q_i * jnp.exp(g_i)[..., None], state)
             + jnp.einsum('bhce,bhev->bhcv', a_i, v_new))
        g_last = g_i[..., -1]
        state = (state * jnp.exp(g_last)[..., None, None]
                 + jnp.einsum('bhcd,bhcv->bhdv', k_i * jnp.exp(g_last[..., None] - g_i)[..., None], v_new))
        return state, o

    state0 = jnp.zeros((bn, h, dk, dv), f32)
    _, o = lax.scan(step, state0, (qc, kc, u, w, attn_intra, gcum))
    return o.transpose(1, 0, 3, 2, 4).reshape(bn, s, h, dv)


def dilated_window_attention(q, k, v, window, dilation, slopes):
    bn, s, h, dh = q.shape
    side = window // (2 * dilation)
    length = s // dilation
    nb = -(-length // B_BLOCK)
    lp = nb * B_BLOCK
    bd = bn * dilation

    def to_phases(t):
        t = t.astype(jnp.float32).reshape(bn, length, dilation, h, dh).transpose(0, 2, 1, 3, 4)
        return jnp.pad(t.reshape(bd, length, h, dh), ((0, 0), (0, lp - length), (0, 0), (0, 0)))

    def neighbours(t):
        tb = jnp.pad(t, ((0, 0), (B_BLOCK, B_BLOCK), (0, 0), (0, 0))).reshape(bd, nb + 2, B_BLOCK, h, dh)
        return jnp.concatenate([tb[:, :-2], tb[:, 1:-1], tb[:, 2:]], axis=2)

    qb = to_phases(q).reshape(bd, nb, B_BLOCK, h, dh)
    kb = neighbours(to_phases(k))
    vb = neighbours(to_phases(v))
    qpos = jnp.arange(nb)[:, None] * B_BLOCK + jnp.arange(B_BLOCK)[None, :]
    kpos = (jnp.arange(nb)[:, None] - 1) * B_BLOCK + jnp.arange(3 * B_BLOCK)[None, :]
    delta = kpos[:, None, :] - qpos[:, :, None]
    valid = (jnp.abs(delta) <= side) & (kpos[:, None, :] >= 0) & (kpos[:, None, :] < length)
    dist = (jnp.abs(delta) * dilation).astype(jnp.float32)
    scores = (jnp.einsum('bnqhd,bnkhd->bhnqk', qb, kb) * (dh ** -0.5)
              - slopes[:, None, None, None] * dist)
    scores = jnp.where(valid, scores, NEG)
    lse = jax.nn.logsumexp(scores, axis=-1)
    p = jnp.exp(scores - lse[..., None])
    o = jnp.einsum('bhnqk,bnkhd->bnqhd', p, vb).reshape(bd, lp, h, dh)[:, :length]
    lse = lse.transpose(0, 2, 3, 1).reshape(bd, lp, h)[:, :length]
    o = o.reshape(bn, dilation, length, h, dh).transpose(0, 2, 1, 3, 4).reshape(bn, s, h, dh)
    lse = lse.reshape(bn, dilation, length, h).transpose(0, 2, 1, 3).reshape(bn, s, h)
    return o, lse


def token_mixer(hn, w_in, conv_w, a_log, dt_bias, a_norm, w_proj_a, w_proj_b, w_out):
    bn, s, _ = hn.shape
    dt = hn.dtype
    proj = jnp.einsum('bsd,dc->bsc', hn, w_in)
    qkv_a, z, beta_raw, alpha_raw, qkv_b, gates = jnp.split(proj, SPLIT_POINTS, axis=-1)

    qkv_a = jax.nn.silu(centred_depthwise_conv(qkv_a, conv_w))
    qa, ka, va = [t.reshape(bn, s, A_HEADS, A_HEAD_DIM) for t in jnp.split(qkv_a, 3, axis=-1)]
    qa, ka = l2norm(qa), l2norm(ka)
    beta = jax.nn.sigmoid(beta_raw.astype(jnp.float32)).reshape(bn, s, 2, A_HEADS)
    g = (-jnp.exp(a_log.astype(jnp.float32))
         * jax.nn.softplus(alpha_raw.astype(jnp.float32).reshape(bn, s, 2, A_HEADS) + dt_bias.astype(jnp.float32)))
    o_fwd = gated_delta_chunked(qa, ka, va, g[:, :, 0], beta[:, :, 0])
    flip = lambda t: jnp.flip(t, axis=1)
    o_bwd = flip(gated_delta_chunked(flip(qa), flip(ka), flip(va), flip(g[:, :, 1]), flip(beta[:, :, 1])))
    oa = rmsnorm(o_fwd + o_bwd, a_norm) * jax.nn.silu(z.astype(jnp.float32).reshape(bn, s, A_HEADS, A_HEAD_DIM))
    ya = jnp.einsum('bsc,cd->bsd', oa.reshape(bn, s, A_WIDTH).astype(dt), w_proj_a)

    qb, kb, vb = [t.reshape(bn, s, B_HEADS, B_HEAD_DIM) for t in jnp.split(qkv_b, 3, axis=-1)]
    slopes = alibi_slopes()
    outs, lses = [], []
    for gi, (window, dilation) in enumerate(B_GROUPS):
        hs = slice(gi * B_HEADS_PER_GROUP, (gi + 1) * B_HEADS_PER_GROUP)
        o_g, lse_g = dilated_window_attention(qb[:, :, hs], kb[:, :, hs], vb[:, :, hs], window, dilation, slopes[hs])
        outs.append(o_g)
        lses.append(lse_g)
    weights = jax.nn.softmax(jnp.stack(lses, axis=0), axis=0)
    ob = jnp.sum(weights[..., None] * jnp.stack(outs, axis=0), axis=0)
    yb = jnp.einsum('bsc,cd->bsd', ob.reshape(bn, s, B_OUT_WIDTH).astype(dt), w_proj_b)

    gate_a, gate_b = jnp.split(jax.nn.sigmoid(gates), 2, axis=-1)
    return jnp.einsum('bsd,de->bse', gate_a * ya + gate_b * yb, w_out)


def expert_choice_ffn(hn, w_router, w_gate, w_up, w_down):
    bn, s, d = hn.shape
    t = bn * s
    cap = (CAPACITY_FACTOR * t) // N_EXPERTS
    xt = hn.reshape(t, d)
    aff = jax.nn.softmax(jnp.einsum('td,de->te', xt, w_router).astype(jnp.float32), axis=-1)
    gate, idx = lax.top_k(aff.T, cap)
    xe = xt[idx]
    hdn = jax.nn.silu(jnp.einsum('ecd,edf->ecf', xe, w_gate)) * jnp.einsum('ecd,edf->ecf', xe, w_up)
    ye = jnp.einsum('ecf,efd->ecd', hdn, w_down) * gate[..., None].astype(hn.dtype)
    y = jnp.zeros_like(xt).at[idx.reshape(-1)].add(ye.reshape(-1, d))
    return y.reshape(bn, s, d)


def run_trunk(x, norm_mix, w_in, conv_w, a_log, dt_bias, a_norm, w_proj_a, w_proj_b, w_out,
              norm_ffn, w_router, w_gate, w_up, w_down, norm_final):
    for l in range(DEPTH):
        x = x + token_mixer(rmsnorm(x, norm_mix[l]), w_in[l], conv_w[l], a_log[l], dt_bias[l], a_norm[l],
                            w_proj_a[l], w_proj_b[l], w_out[l])
        x = x + expert_choice_ffn(rmsnorm(x, norm_ffn[l]), w_router[l], w_gate[l], w_up[l], w_down[l])
    return rmsnorm(x, norm_final)


def setup_inputs(seed: int = 0) -> dict:
    key = jax.random.key(seed)
    ks = jax.random.split(key, 20)
    f32 = jnp.float32
    nrm = lambda k, shape, scale: jax.random.normal(k, shape, f32) * scale
    dt0 = jnp.exp(jax.random.uniform(ks[6], (DEPTH, 2, A_HEADS), f32, math.log(1e-3), math.log(1e-1)))
    return {
        "x_prompt": nrm(ks[0], (BATCH, SEQ, D_MODEL), 1.0),
        "x_sample": nrm(ks[1], (DEC_BATCH, DEC_SEQ, D_MODEL), 1.0),
        "norm_mix": 1.0 + nrm(ks[2], (DEPTH, D_MODEL), 0.02),
        "w_in": nrm(ks[3], (DEPTH, D_MODEL, IN_WIDTH), D_MODEL ** -0.5),
        "conv_w": nrm(ks[4], (DEPTH, A_CONV, 3 * A_WIDTH), A_CONV ** -0.5),
        "a_log": jnp.log(jax.random.uniform(ks[5], (DEPTH, 2, A_HEADS), f32, 1.0, 16.0)),
        "dt_bias": dt0 + jnp.log(-jnp.expm1(-dt0)),
        "a_norm": 1.0 + nrm(ks[7], (DEPTH, A_HEAD_DIM), 0.02),
        "w_proj_a": nrm(ks[8], (DEPTH, A_WIDTH, D_MODEL), A_WIDTH ** -0.5),
        "w_proj_b": nrm(ks[9], (DEPTH, B_OUT_WIDTH, D_MODEL), B_OUT_WIDTH ** -0.5),
        "w_out": nrm(ks[10], (DEPTH, D_MODEL, D_MODEL), D_MODEL ** -0.5),
        "norm_ffn": 1.0 + nrm(ks[11], (DEPTH, D_MODEL), 0.02),
        "w_router": nrm(ks[12], (DEPTH, D_MODEL, N_EXPERTS), D_MODEL ** -0.5),
        "w_gate": nrm(ks[13], (DEPTH, N_EXPERTS, D_MODEL, D_EXPERT), D_MODEL ** -0.5),
        "w_up": nrm(ks[14], (DEPTH, N_EXPERTS, D_MODEL, D_EXPERT), D_MODEL ** -0.5),
        "w_down": nrm(ks[15], (DEPTH, N_EXPERTS, D_EXPERT, D_MODEL), D_EXPERT ** -0.5),
        "norm_final": 1.0 + nrm(ks[16], (D_MODEL,), 0.02),
    }


def reference(x_prompt, x_sample, norm_mix, w_in, conv_w, a_log, dt_bias, a_norm, w_proj_a, w_proj_b,
              w_out, norm_ffn, w_router, w_gate, w_up, w_down, norm_final):
    y_prompt = run_trunk(x_prompt, norm_mix, w_in, conv_w, a_log, dt_bias, a_norm, w_proj_a, w_proj_b, w_out,
                         norm_ffn, w_router, w_gate, w_up, w_down, norm_final)
    y_sample = run_trunk(x_sample, norm_mix, w_in, conv_w, a_log, dt_bias, a_norm, w_proj_a, w_proj_b, w_out,
                         norm_ffn, w_router, w_gate, w_up, w_down, norm_final)
    return (y_prompt, y_sample)
```

```python
import functools

import numpy as np
import jax
import jax.numpy as jnp
from jax import lax
from jax.experimental import pallas as pl
from jax.experimental.pallas import tpu as pltpu

f32 = jnp.float32
bf16 = jnp.bfloat16
i32 = jnp.int32
HIGHEST = lax.Precision.HIGHEST

D_MODEL = 1024
A_HEADS = 8
HEAD_DIM = 64
A_WIDTH = A_HEADS * HEAD_DIM
A_CONV = 5
CHUNK = 64
B_GROUPS = ((128, 1), (512, 4), (2048, 16))
B_HEADS_PER_GROUP = 4
B_HEADS = B_HEADS_PER_GROUP * len(B_GROUPS)
B_GROUP_WIDTH = B_HEADS_PER_GROUP * HEAD_DIM
B_SIDE = 64
N_EXPERTS = 16
D_EXPERT = 1024
CAPACITY_FACTOR = 2
EPS = 1e-6
NEG = -1e30

LANES = 128
MAIN_WIDTH = 2 * D_MODEL + 4 * A_WIDTH + 3 * B_HEADS * HEAD_DIM
QKV_A_BLK = 2 * D_MODEL // LANES
Z_BLK = QKV_A_BLK + 3 * A_WIDTH // LANES
QKV_B_BLK256 = (2 * D_MODEL + 4 * A_WIDTH) // B_GROUP_WIDTH
MAIN_BLK256 = MAIN_WIDTH // B_GROUP_WIDTH
N_TILE = 1280
SEL_TILE = 256
SMALL_BUCKET = 64
ROW_ALIGN = 16
VMEM_LIMIT = 56 * 1024 * 1024


def _cparams(sem):
    return pltpu.CompilerParams(dimension_semantics=sem, vmem_limit_bytes=VMEM_LIMIT)


def _sigmoid(x):
    return 1.0 / (1.0 + jnp.exp(-x))


def _softplus(x):
    return jnp.maximum(x, 0.0) + jnp.log(1.0 + jnp.exp(-jnp.abs(x)))


def _inproj_kernel(x_ref, g_ref, w_ref, ws_ref, o_ref, os_ref, n_ref):
    @pl.when(pl.program_id(1) == 0)
    def _():
        x = x_ref[...]
        ms = jnp.mean(x * x, axis=-1, keepdims=True)
        n = (x * lax.rsqrt(ms + EPS) * g_ref[...]).astype(bf16)
        n_ref[...] = n
        os_ref[...] = jnp.dot(n, ws_ref[...], preferred_element_type=f32)

    o_ref[...] = jnp.dot(n_ref[...], w_ref[...], preferred_element_type=f32).astype(o_ref.dtype)


def _inproj(x2, g, w_main, w_small):
    t = x2.shape[0]
    tm = min(1024, t)
    return pl.pallas_call(
        _inproj_kernel,
        grid=(t // tm, MAIN_WIDTH // N_TILE),
        in_specs=[
            pl.BlockSpec((tm, D_MODEL), lambda i, j: (i, 0)),
            pl.BlockSpec((1, D_MODEL), lambda i, j: (0, 0)),
            pl.BlockSpec((D_MODEL, N_TILE), lambda i, j: (0, j)),
            pl.BlockSpec((D_MODEL, LANES), lambda i, j: (0, 0)),
        ],
        out_specs=[
            pl.BlockSpec((tm, N_TILE), lambda i, j: (i, j)),
            pl.BlockSpec((tm, LANES), lambda i, j: (i, 0)),
        ],
        out_shape=[jax.ShapeDtypeStruct((t, MAIN_WIDTH), bf16), jax.ShapeDtypeStruct((t, LANES), f32)],
        scratch_shapes=[pltpu.VMEM((tm, D_MODEL), bf16)],
        compiler_params=_cparams(("parallel", "arbitrary")),
        name="inproj",
    )(x2, g, w_main, w_small)


HALO = 16


def _convprep_kernel(cur_ref, prev_ref, next_ref, w_ref, o_ref, *, ts, n_tiles):
    i = pl.program_id(1)
    c = pl.program_id(2)
    prev = prev_ref[0].astype(f32) * (i > 0).astype(f32)
    nxt = next_ref[0].astype(f32) * (i < n_tiles - 1).astype(f32)
    a = jnp.concatenate([prev, cur_ref[0].astype(f32), nxt], axis=0)
    w = w_ref[...]
    pad = (A_CONV - 1) // 2
    y = jnp.zeros((ts, LANES), f32)
    for j in range(A_CONV):
        off = HALO - pad + j
        y = y + a[off:off + ts] * w[j:j + 1]
    y = y * _sigmoid(y)
    r = lax.broadcasted_iota(i32, (LANES, LANES), 0) // HEAD_DIM
    cc = lax.broadcasted_iota(i32, (LANES, LANES), 1) // HEAD_DIM
    head_ones = (r == cc).astype(f32)
    ss = jnp.dot(y * y, head_ones, preferred_element_type=f32, precision=HIGHEST)
    yn = y * lax.rsqrt(ss + EPS)
    is_qk = c < 2 * A_WIDTH // LANES
    qscale = jnp.where(c < A_WIDTH // LANES, HEAD_DIM ** -0.5, 1.0).astype(f32)
    o_ref[0] = jnp.where(is_qk, yn * qscale, y).astype(o_ref.dtype)


def _convprep(main3, conv_w):
    b, s, _ = main3.shape
    ts = min(1024, s)
    n_tiles = s // ts
    hb = ts // HALO
    nblk = 3 * A_WIDTH // LANES
    return pl.pallas_call(
        functools.partial(_convprep_kernel, ts=ts, n_tiles=n_tiles),
        grid=(b, n_tiles, nblk),
        in_specs=[
            pl.BlockSpec((1, ts, LANES), lambda bi, i, c: (bi, i, QKV_A_BLK + c)),
            pl.BlockSpec((1, HALO, LANES), lambda bi, i, c: (bi, jnp.maximum(i * hb - 1, 0), QKV_A_BLK + c)),
            pl.BlockSpec((1, HALO, LANES),
                         lambda bi, i, c: (bi, jnp.minimum((i + 1) * hb, s // HALO - 1), QKV_A_BLK + c)),
            pl.BlockSpec((A_CONV, LANES), lambda bi, i, c: (0, c)),
        ],
        out_specs=pl.BlockSpec((1, ts, LANES), lambda bi, i, c: (bi, i, c)),
        out_shape=jax.ShapeDtypeStruct((b, s, 3 * A_WIDTH), bf16),
        compiler_params=_cparams(("parallel", "parallel", "parallel")),
        name="convprep",
    )(main3, main3, main3, conv_w)


def _delta_direction(q, k, v, gq, gfull, bfull, pair, upper, s_ref):
    c = CHUNK
    d = 1 if upper else 0
    lane = lax.broadcasted_iota(i32, (c, LANES), 1)
    gcols, bcols = [], []
    for hh in range(2):
        h = 2 * pair + hh
        gcol = jnp.sum(jnp.where(lane == 2 * A_HEADS + d * A_HEADS + h, gfull, 0.0), axis=1, keepdims=True)
        bcol = jnp.sum(jnp.where(lane == d * A_HEADS + h, bfull, 0.0), axis=1, keepdims=True)
        gcols.append(jnp.broadcast_to(gcol, (c, c)))
        bcols.append(jnp.broadcast_to(bcol, (c, c)))
    gst = jnp.concatenate(gcols, axis=0)
    bst = jnp.concatenate(bcols, axis=0)
    r2 = lax.broadcasted_iota(i32, (2 * c, 2 * c), 0)
    c2 = lax.broadcasted_iota(i32, (2 * c, 2 * c), 1)
    same = (r2 // c) == (c2 // c)
    cum_mat = (same & ((r2 <= c2) if upper else (r2 >= c2))).astype(f32)
    cst = jnp.dot(cum_mat, gst, preferred_element_type=f32, precision=HIGHEST)
    rst = jnp.concatenate([cst[0:c].T, cst[c:2 * c].T], axis=0)
    ri = lax.broadcasted_iota(i32, (2 * c, c), 0) % c
    ci = lax.broadcasted_iota(i32, (2 * c, c), 1)
    incl = (ri <= ci) if upper else (ri >= ci)
    strict = (ri < ci) if upper else (ri > ci)
    dst = jnp.where(incl, jnp.exp(jnp.minimum(cst - rst, 0.0)), 0.0)
    mst = jnp.where(strict, -(bst * gq[2 * c:4 * c] * dst), 0.0)
    attn = (gq[0:2 * c] * dst).astype(bf16)
    eye = (lax.broadcasted_iota(i32, (c, c), 0) == lax.broadcasted_iota(i32, (c, c), 1)).astype(f32)

    tinv = []
    for hh in range(2):
        a = mst[hh * c:(hh + 1) * c]
        p = eye + a
        a = jnp.dot(a.astype(bf16), a.astype(bf16), preferred_element_type=f32)
        n_steps = int(np.log2(c)) - 1
        for step in range(n_steps):
            ab = a.astype(bf16)
            if step < n_steps - 1:
                y = jnp.dot(jnp.concatenate([a, p], axis=0).astype(bf16), ab, preferred_element_type=f32)
                a = y[0:c]
                p = p + y[c:2 * c]
            else:
                p = p + jnp.dot(p.astype(bf16), ab, preferred_element_type=f32)
        tinv.append(p.astype(bf16))

    if upper:
        gtot = [cst[0:1], cst[c:c + 1]]
    else:
        gtot = [cst[c - 1:c], cst[2 * c - 1:2 * c]]
    pack = lambda x0, x1: jnp.concatenate([x0, x1], axis=1)
    beta_p = pack(bst[0:c], bst[c:2 * c])
    eg_p = jnp.exp(pack(cst[0:c], cst[c:2 * c]))
    kd_p = jnp.exp(pack(gtot[0] - cst[0:c], gtot[1] - cst[c:2 * c]))
    dec_p = jnp.exp(pack(gtot[0], gtot[1]))
    kf = k.astype(f32)
    kb = kf * beta_p
    rhs = jnp.concatenate([v.astype(f32) * beta_p, kb * eg_p], axis=1).astype(bf16)
    y0 = jnp.dot(tinv[0], rhs, preferred_element_type=f32)
    y1 = jnp.dot(tinv[1], rhs, preferred_element_type=f32)
    m0 = lane < HEAD_DIM
    u_p = jnp.where(m0, y0[:, 0:LANES], y1[:, 0:LANES])
    w_p = jnp.where(m0, y0[:, LANES:2 * LANES], y1[:, LANES:2 * LANES])
    state = s_ref[...]
    sb = state.astype(bf16)
    qg = q.astype(f32) * eg_p
    ws = jnp.dot(jnp.concatenate([w_p, qg], axis=0).astype(bf16), sb, preferred_element_type=f32)
    v_new = u_p - ws[0:c]
    vb = v_new.astype(bf16)
    o = ws[c:2 * c] + jnp.where(m0,
                                jnp.dot(attn[0:c], vb, preferred_element_type=f32),
                                jnp.dot(attn[c:2 * c], vb, preferred_element_type=f32))
    kd = (kf * kd_p).astype(bf16)
    upd = lax.dot_general(kd, vb, (((0,), (0,)), ((), ())), preferred_element_type=f32)
    rb = lax.broadcasted_iota(i32, (LANES, LANES), 0) // HEAD_DIM
    cb = lax.broadcasted_iota(i32, (LANES, LANES), 1) // HEAD_DIM
    s_ref[...] = state * dec_p + jnp.where(rb == cb, upd, 0.0)
    return o


def _delta_kernel(q_ref, k_ref, v_ref, z_ref, sm_ref, par_ref, an_ref, o_ref, acc_ref, sf_ref, sb_ref, *, s):
    c = CHUNK
    n = s // c
    pair = pl.program_id(1)
    acc_ref[...] = jnp.zeros_like(acc_ref)
    sf_ref[...] = jnp.zeros_like(sf_ref)
    sb_ref[...] = jnp.zeros_like(sb_ref)
    lane = lax.broadcasted_iota(i32, (c, LANES), 1)
    m0 = lane < HEAD_DIM
    a_row = par_ref[0:1, :]
    dt_row = par_ref[1:2, :]

    def one(cidx, upper, s_ref):
        r0 = pl.multiple_of(cidx * c, c)
        q = q_ref[0, pl.ds(r0, c), :]
        k = k_ref[0, pl.ds(r0, c), :]
        v = v_ref[0, pl.ds(r0, c), :]
        sm = sm_ref[0, pl.ds(r0, c), :]
        gfull = -a_row * _softplus(sm + dt_row)
        bfull = _sigmoid(sm)
        zq = jnp.zeros_like(q)
        lhs = jnp.concatenate([jnp.where(m0, q, zq), jnp.where(m0, zq, q),
                               jnp.where(m0, k, zq), jnp.where(m0, zq, k)], axis=0)
        gq = lax.dot_general(lhs, k, (((1,), (1,)), ((), ())), preferred_element_type=f32)
        o = _delta_direction(q, k, v, gq, gfull, bfull, pair, upper, s_ref)
        acc_ref[pl.ds(r0, c), :] += o

    def body(i, carry):
        one(i, False, sf_ref)
        one(n - 1 - i, True, sb_ref)
        return carry

    lax.fori_loop(0, n, body, 0)

    rows = min(256, s)
    r = lax.broadcasted_iota(i32, (LANES, LANES), 0) // HEAD_DIM
    cc = lax.broadcasted_iota(i32, (LANES, LANES), 1) // HEAD_DIM
    head_mean = (r == cc).astype(f32) * (1.0 / HEAD_DIM)

    def epi(i, carry):
        r0 = pl.multiple_of(i * rows, rows)
        o = acc_ref[pl.ds(r0, rows), :]
        ms = jnp.dot(o * o, head_mean, preferred_element_type=f32, precision=HIGHEST)
        z = z_ref[0, pl.ds(r0, rows), :].astype(f32)
        y = o * lax.rsqrt(ms + EPS) * an_ref[...] * (z * _sigmoid(z))
        o_ref[0, pl.ds(r0, rows), :] = y.astype(o_ref.dtype)
        return carry

    lax.fori_loop(0, s // rows, epi, 0)


def _delta(qkvn, main3, small3, par, an):
    b, s, _ = qkvn.shape
    npair = A_HEADS // 2
    kb = A_WIDTH // LANES
    seq = lambda off: pl.BlockSpec((1, s, LANES), lambda bi, p: (bi, 0, off + p))
    return pl.pallas_call(
        functools.partial(_delta_kernel, s=s),
        grid=(b, npair),
        in_specs=[
            seq(0), seq(kb), seq(2 * kb), seq(Z_BLK),
            pl.BlockSpec((1, s, LANES), lambda bi, p: (bi, 0, 0)),
            pl.BlockSpec((8, LANES), lambda bi, p: (0, 0)),
            pl.BlockSpec((1, LANES), lambda bi, p: (0, 0)),
        ],
        out_specs=pl.BlockSpec((1, s, LANES), lambda bi, p: (bi, 0, p)),
        out_shape=jax.ShapeDtypeStruct((b, s, A_WIDTH), bf16),
        scratch_shapes=[pltpu.VMEM((s, LANES), f32), pltpu.VMEM((LANES, LANES), f32),
                        pltpu.VMEM((LANES, LANES), f32)],
        compiler_params=_cparams(("parallel", "arbitrary")),
        name="delta",
    )(qkvn, qkvn, qkvn, main3, small3, par, an)


def _attn_kernel(q_ref, k_ref, v_ref, o_ref, l_ref, *, lp, bq, nk, dil, slopes):
    w = B_GROUP_WIDTH
    head_of_lane = lax.broadcasted_iota(i32, (1, w), 1) // HEAD_DIM
    scale = HEAD_DIM ** -0.5

    def body(i, carry):
        q0 = pl.multiple_of(i * bq, bq)
        ks = pl.multiple_of(jnp.clip(q0 - B_SIDE, 0, lp - nk), B_SIDE)
        q = q_ref[0, pl.ds(q0, bq), :]
        k = k_ref[0, pl.ds(ks, nk), :]
        v = v_ref[0, pl.ds(ks, nk), :]
        zq = jnp.zeros_like(q)
        qs = jnp.concatenate([jnp.where(head_of_lane == h, q, zq) for h in range(B_HEADS_PER_GROUP)], axis=0)
        sc = lax.dot_general(qs, k, (((1,), (1,)), ((), ())), preferred_element_type=f32) * scale
        qpos = q0 + lax.broadcasted_iota(i32, (bq, nk), 0)
        kpos = ks + lax.broadcasted_iota(i32, (bq, nk), 1)
        adelta = jnp.abs(kpos - qpos)
        valid = adelta <= B_SIDE
        dist = adelta.astype(f32) * float(dil)
        o = jnp.zeros((bq, w), f32)
        lse = jnp.zeros((bq, w), f32)
        for h in range(B_HEADS_PER_GROUP):
            sh = jnp.where(valid, sc[h * bq:(h + 1) * bq] - slopes[h] * dist, NEG)
            m = jnp.max(sh, axis=1, keepdims=True)
            p = jnp.exp(sh - m)
            l = jnp.sum(p, axis=1, keepdims=True)
            oh = jnp.dot(p.astype(bf16), v, preferred_element_type=f32) / l
            hm = head_of_lane == h
            o = jnp.where(hm, oh, o)
            lse = jnp.where(hm, m + jnp.log(l), lse)
        o_ref[0, pl.ds(q0, bq), :] = o.astype(o_ref.dtype)
        l_ref[0, pl.ds(q0, bq), :] = lse
        return carry

    lax.fori_loop(0, lp // bq, body, 0)


def _attention(main3, group):
    b, s, _ = main3.shape
    _, dil = B_GROUPS[group]
    lp = s // dil
    bq = min(128, lp)
    nk = min(bq + 2 * B_SIDE, lp)
    slopes = tuple(float(2.0 ** (-8.0 * (group * B_HEADS_PER_GROUP + h + 1) / B_HEADS))
                   for h in range(B_HEADS_PER_GROUP))
    w = B_GROUP_WIDTH
    mainp = main3.reshape(b, lp, dil * MAIN_WIDTH)
    ngrp = len(B_GROUPS)
    spec = lambda off: pl.BlockSpec((1, lp, w), lambda bi, r: (bi, 0, r * MAIN_BLK256 + QKV_B_BLK256 + off + group))
    o, lse = pl.pallas_call(
        functools.partial(_attn_kernel, lp=lp, bq=bq, nk=nk, dil=dil, slopes=slopes),
        grid=(b, dil),
        in_specs=[spec(0), spec(ngrp), spec(2 * ngrp)],
        out_specs=[pl.BlockSpec((1, lp, w), lambda bi, r: (bi, 0, r)),
                   pl.BlockSpec((1, lp, w), lambda bi, r: (bi, 0, r))],
        out_shape=[jax.ShapeDtypeStruct((b, lp, dil * w), bf16), jax.ShapeDtypeStruct((b, lp, dil * w), f32)],
        compiler_params=_cparams(("parallel", "parallel")),
        name=f"attn_d{dil}",
    )(mainp, mainp, mainp)
    return o.reshape(b * s, w), lse.reshape(b * s, w)


def _merge_kernel(x_ref, oa_ref, o1_ref, o2_ref, o3_ref, l1_ref, l2_ref, l3_ref, ga_ref, gb_ref,
                  wpa_ref, wpb_ref, wout_ref, out_ref):
    l1, l2, l3 = l1_ref[...], l2_ref[...], l3_ref[...]
    m = jnp.maximum(jnp.maximum(l1, l2), l3)
    e1, e2, e3 = jnp.exp(l1 - m), jnp.exp(l2 - m), jnp.exp(l3 - m)
    ob = (e1 * o1_ref[...].astype(f32) + e2 * o2_ref[...].astype(f32) + e3 * o3_ref[...].astype(f32)) / (e1 + e2 + e3)
    ya = jnp.dot(oa_ref[...], wpa_ref[...], preferred_element_type=f32)
    yb = jnp.dot(ob.astype(bf16), wpb_ref[...], preferred_element_type=f32)
    mix = _sigmoid(ga_ref[...].astype(f32)) * ya + _sigmoid(gb_ref[...].astype(f32)) * yb
    out_ref[...] = x_ref[...] + jnp.dot(mix.astype(bf16), wout_ref[...], preferred_element_type=f32)


def _merge(x2, oa2, obs, lses, main2, wpa, wpb, wout):
    t = x2.shape[0]
    tm = min(512, t)
    w = B_GROUP_WIDTH
    row = lambda width: pl.BlockSpec((tm, width), lambda i: (i, 0))
    full = lambda a, bb: pl.BlockSpec((a, bb), lambda i: (0, 0))
    return pl.pallas_call(
        _merge_kernel,
        grid=(t // tm,),
        in_specs=[row(D_MODEL), row(A_WIDTH), row(w), row(w), row(w), row(w), row(w), row(w),
                  pl.BlockSpec((tm, D_MODEL), lambda i: (i, 0)), pl.BlockSpec((tm, D_MODEL), lambda i: (i, 1)),
                  full(A_WIDTH, D_MODEL), full(w, D_MODEL), full(D_MODEL, D_MODEL)],
        out_specs=row(D_MODEL),
        out_shape=jax.ShapeDtypeStruct((t, D_MODEL), f32),
        compiler_params=_cparams(("parallel",)),
        name="merge",
    )(x2, oa2, *obs, *lses, main2, main2, wpa, wpb, wout)


def _router_kernel(x_ref, g_ref, wr_ref, hn_ref, aff_ref, afft_ref):
    x = x_ref[...]
    ms = jnp.mean(x * x, axis=-1, keepdims=True)
    hn = x * lax.rsqrt(ms + EPS) * g_ref[...]
    hn_ref[...] = hn.astype(bf16)
    logits = jnp.dot(hn, wr_ref[...], preferred_element_type=f32, precision=HIGHEST)
    lane = lax.broadcasted_iota(i32, logits.shape, 1)
    logits = jnp.where(lane < N_EXPERTS, logits, NEG)
    m = jnp.max(logits, axis=1, keepdims=True)
    e = jnp.exp(logits - m)
    aff = e / jnp.sum(e, axis=1, keepdims=True)
    aff_ref[...] = aff
    afft_ref[...] = aff.T[0:N_EXPERTS]


def _router(x2, g, wr):
    t = x2.shape[0]
    tm = min(512, t)
    return pl.pallas_call(
        _router_kernel,
        grid=(t // tm,),
        in_specs=[pl.BlockSpec((tm, D_MODEL), lambda i: (i, 0)),
                  pl.BlockSpec((1, D_MODEL), lambda i: (0, 0)),
                  pl.BlockSpec((D_MODEL, LANES), lambda i: (0, 0))],
        out_specs=[pl.BlockSpec((tm, D_MODEL), lambda i: (i, 0)),
                   pl.BlockSpec((tm, LANES), lambda i: (i, 0)),
                   pl.BlockSpec((N_EXPERTS, tm), lambda i: (0, i))],
        out_shape=[jax.ShapeDtypeStruct((t, D_MODEL), bf16), jax.ShapeDtypeStruct((t, LANES), f32),
                   jax.ShapeDtypeStruct((N_EXPERTS, t), f32)],
        compiler_params=_cparams(("parallel",)),
        name="router",
    )(x2, g, wr)


def _threshold_kernel(afft_ref, thr_ref, need_ref, *, cap):
    bits = lax.bitcast_convert_type(afft_ref[...], i32)

    def step(i, lo):
        cand = lo | lax.shift_left(jnp.int32(1), 30 - i)
        cnt = jnp.sum((bits >= cand).astype(i32), axis=1, keepdims=True)
        return jnp.where(cnt >= cap, cand, lo)

    thr = lax.fori_loop(0, 31, step, jnp.zeros((N_EXPERTS, 1), i32))
    n_gt = jnp.sum((bits > thr).astype(i32), axis=1, keepdims=True)
    thr_ref[...] = jnp.broadcast_to(thr, (N_EXPERTS, LANES))
    need_ref[...] = jnp.broadcast_to(cap - n_gt, (N_EXPERTS, LANES))


def _threshold(afft, cap):
    return pl.pallas_call(
        functools.partial(_threshold_kernel, cap=cap),
        out_shape=[jax.ShapeDtypeStruct((N_EXPERTS, LANES), i32), jax.ShapeDtypeStruct((N_EXPERTS, LANES), i32)],
        compiler_params=pltpu.CompilerParams(vmem_limit_bytes=VMEM_LIMIT),
        name="threshold",
    )(afft)


def _select_kernel(aff_ref, thr_ref, need_ref, rk_ref, tb_ref, carry_ref):
    tt = SEL_TILE

    @pl.when(pl.program_id(0) == 0)
    def _():
        carry_ref[...] = jnp.zeros_like(carry_ref)

    bits = lax.bitcast_convert_type(aff_ref[...], i32)
    lane_ok = lax.broadcasted_iota(i32, (tt, LANES), 1) < N_EXPERTS
    thr = thr_ref[...]
    gt = (bits > thr) & lane_ok
    eq = (bits == thr) & lane_ok
    below = (lax.broadcasted_iota(i32, (tt, tt), 0) > lax.broadcasted_iota(i32, (tt, tt), 1)).astype(bf16)
    tie_carry = carry_ref[0:1, :]
    base = carry_ref[1:2, :]
    eqf = eq.astype(f32)
    tie_before = jnp.dot(below, eq.astype(bf16), preferred_element_type=f32) + tie_carry
    sel = gt | (eq & (tie_before < need_ref[...].astype(f32)))
    self_ = sel.astype(f32)
    rank = jnp.dot(below, sel.astype(bf16), preferred_element_type=f32)
    n = jnp.sum(self_, axis=0, keepdims=True)
    rk_ref[...] = jnp.where(sel, rank, -1.0)
    tb_ref[0, 0:1, :] = base.astype(i32)
    tb_ref[0, 1:2, :] = n.astype(i32)
    tb_ref[0, 2:8, :] = jnp.zeros((6, LANES), i32)
    carry_ref[0:1, :] = tie_carry + jnp.sum(eqf, axis=0, keepdims=True)
    carry_ref[1:2, :] = base + n


def _select(aff, thr_row, need_row):
    t = aff.shape[0]
    nt = t // SEL_TILE
    return pl.pallas_call(
        _select_kernel,
        grid=(nt,),
        in_specs=[pl.BlockSpec((SEL_TILE, LANES), lambda j: (j, 0)),
                  pl.BlockSpec((1, LANES), lambda j: (0, 0)),
                  pl.BlockSpec((1, LANES), lambda j: (0, 0))],
        out_specs=[pl.BlockSpec((SEL_TILE, LANES), lambda j: (j, 0)),
                   pl.BlockSpec((1, 8, LANES), lambda j: (j, 0, 0))],
        out_shape=[jax.ShapeDtypeStruct((t, LANES), f32), jax.ShapeDtypeStruct((nt, 8, LANES), i32)],
        scratch_shapes=[pltpu.VMEM((8, LANES), f32)],
        compiler_params=_cparams(("arbitrary",)),
        name="select",
    )(aff, thr_row, need_row)


def _buckets(n):
    return ((SMALL_BUCKET + ROW_ALIGN, (n > 0) & (n <= SMALL_BUCKET)), (SEL_TILE + ROW_ALIGN, n > SMALL_BUCKET))


def _dispatch_copy(stage_ref, xe_ref, sem_ref, e, start, rows):
    return pltpu.make_async_copy(stage_ref.at[e, pl.ds(0, rows)], xe_ref.at[e, pl.ds(start, rows)], sem_ref.at[e])


def _dispatch_kernel(base_ref, cnt_ref, rk_ref, hn_ref, xe_ref, stage_ref, carry_ref, sem_ref):
    j = pl.program_id(0)
    tt = SEL_TILE

    @pl.when(j == 0)
    def _():
        carry_ref[...] = jnp.zeros_like(carry_ref)

    rkt = rk_ref[...].T
    hn = hn_ref[...]
    for e in range(N_EXPERTS):
        n = cnt_ref[j * N_EXPERTS + e]
        base = base_ref[j * N_EXPERTS + e]
        rem = base % ROW_ALIGN
        start = pl.multiple_of(base - rem, ROW_ALIGN)
        row = rkt[e:e + 1, :]
        tgt = jnp.where(row >= 0.0, row + rem.astype(f32), -1.0)
        for rows, cond in _buckets(n):
            @pl.when(cond)
            def _(rows=rows, start=start, tgt=tgt, e=e, rem=rem, n=n):
                slot = lax.broadcasted_iota(i32, (rows, tt), 0).astype(f32)
                onehot = (slot == tgt).astype(bf16)
                stage_ref[e, 0:rows, :] = jnp.dot(onehot, hn, preferred_element_type=f32).astype(bf16)
                stage_ref[e, 0:ROW_ALIGN, :] += carry_ref[e]
                keep = pl.multiple_of(((rem + n) // ROW_ALIGN) * ROW_ALIGN, ROW_ALIGN)
                carry_ref[e] = stage_ref[e, pl.ds(keep, ROW_ALIGN), :]
                _dispatch_copy(stage_ref, xe_ref, sem_ref, e, start, rows).start()
    for e in range(N_EXPERTS):
        n = cnt_ref[j * N_EXPERTS + e]
        base = base_ref[j * N_EXPERTS + e]
        start = pl.multiple_of(base - base % ROW_ALIGN, ROW_ALIGN)
        for rows, cond in _buckets(n):
            @pl.when(cond)
            def _(rows=rows, start=start, e=e):
                _dispatch_copy(stage_ref, xe_ref, sem_ref, e, start, rows).wait()


def _dispatch(base, cnt, rk, hn, cap):
    t = hn.shape[0]
    nt = t // SEL_TILE
    return pl.pallas_call(
        _dispatch_kernel,
        grid_spec=pltpu.PrefetchScalarGridSpec(
            num_scalar_prefetch=2,
            grid=(nt,),
            in_specs=[pl.BlockSpec((SEL_TILE, LANES), lambda j, b, c: (j, 0)),
                      pl.BlockSpec((SEL_TILE, D_MODEL), lambda j, b, c: (j, 0))],
            out_specs=pl.BlockSpec(memory_space=pl.ANY),
            scratch_shapes=[pltpu.VMEM((N_EXPERTS, SEL_TILE + ROW_ALIGN, D_MODEL), bf16),
                            pltpu.VMEM((N_EXPERTS, ROW_ALIGN, D_MODEL), bf16),
                            pltpu.SemaphoreType.DMA((N_EXPERTS,))],
        ),
        out_shape=jax.ShapeDtypeStruct((N_EXPERTS, cap + SEL_TILE + ROW_ALIGN, D_MODEL), bf16),
        compiler_params=_cparams(("arbitrary",)),
        name="dispatch",
    )(base, cnt, rk, hn)


def _ffn_kernel(x_ref, wg_ref, wu_ref, wd_ref, y_ref):
    x = x_ref[0]
    g = jnp.dot(x, wg_ref[0], preferred_element_type=f32)
    u = jnp.dot(x, wu_ref[0], preferred_element_type=f32)
    h = (g * _sigmoid(g) * u).astype(bf16)
    y_ref[0] = jnp.dot(h, wd_ref[0], preferred_element_type=f32).astype(y_ref.dtype)


def _ffn(xe, wg, wu, wd, cap):
    tr = min(512, cap)
    wspec = lambda a, bb: pl.BlockSpec((1, a, bb), lambda e, i: (e, 0, 0))
    return pl.pallas_call(
        _ffn_kernel,
        grid=(N_EXPERTS, cap // tr),
        in_specs=[pl.BlockSpec((1, tr, D_MODEL), lambda e, i: (e, i, 0)),
                  wspec(D_MODEL, D_EXPERT), wspec(D_MODEL, D_EXPERT), wspec(D_EXPERT, D_MODEL)],
        out_specs=pl.BlockSpec((1, tr, D_MODEL), lambda e, i: (e, i, 0)),
        out_shape=jax.ShapeDtypeStruct((N_EXPERTS, cap, D_MODEL), bf16),
        compiler_params=_cparams(("parallel", "parallel")),
        name="expert_ffn",
    )(xe, wg, wu, wd)


def _combine_copy(ye_ref, buf_ref, sem_ref, e, start, rows):
    return pltpu.make_async_copy(ye_ref.at[e, pl.ds(start, rows)], buf_ref.at[e, pl.ds(0, rows)], sem_ref.at[e])


def _combine_kernel(base_ref, cnt_ref, rk_ref, aff_ref, x_ref, gf_ref, ye_ref, out_ref, buf_ref, sem_ref,
                    *, cap, final_norm):
    j = pl.program_id(0)
    tt = SEL_TILE
    buckets = lambda n: tuple((min(rows, cap), cond) for rows, cond in _buckets(n))
    window_start = lambda base, rows: pl.multiple_of(jnp.minimum(base - base % ROW_ALIGN, cap - rows), ROW_ALIGN)
    for e in range(N_EXPERTS):
        n = cnt_ref[j * N_EXPERTS + e]
        base = base_ref[j * N_EXPERTS + e]
        for rows, cond in buckets(n):
            @pl.when(cond)
            def _(rows=rows, base=base, e=e):
                _combine_copy(ye_ref, buf_ref, sem_ref, e, window_start(base, rows), rows).start()
    out_ref[...] = x_ref[...]
    for e in range(N_EXPERTS):
        n = cnt_ref[j * N_EXPERTS + e]
        base = base_ref[j * N_EXPERTS + e]
        for rows, cond in buckets(n):
            @pl.when(cond)
            def _(rows=rows, base=base, e=e):
                start = window_start(base, rows)
                _combine_copy(ye_ref, buf_ref, sem_ref, e, start, rows).wait()
                col = rk_ref[:, e:e + 1]
                tgt = col + (base - start).astype(f32)
                slot = lax.broadcasted_iota(i32, (tt, rows), 1).astype(f32)
                onehot = ((tgt == slot) & (col >= 0.0)).astype(bf16)
                contrib = jnp.dot(onehot, buf_ref[e, 0:rows, :], preferred_element_type=f32)
                out_ref[...] += contrib * aff_ref[:, e:e + 1]
    if final_norm:
        y = out_ref[...]
        ms = jnp.mean(y * y, axis=-1, keepdims=True)
        out_ref[...] = y * lax.rsqrt(ms + EPS) * gf_ref[...]


def _combine(base, cnt, rk, aff, x2, gfin, ye, cap, final_norm):
    t = x2.shape[0]
    nt = t // SEL_TILE
    return pl.pallas_call(
        functools.partial(_combine_kernel, cap=cap, final_norm=final_norm),
        grid_spec=pltpu.PrefetchScalarGridSpec(
            num_scalar_prefetch=2,
            grid=(nt,),
            in_specs=[pl.BlockSpec((SEL_TILE, LANES), lambda j, b, c: (j, 0)),
                      pl.BlockSpec((SEL_TILE, LANES), lambda j, b, c: (j, 0)),
                      pl.BlockSpec((SEL_TILE, D_MODEL), lambda j, b, c: (j, 0)),
                      pl.BlockSpec((1, D_MODEL), lambda j, b, c: (0, 0)),
                      pl.BlockSpec(memory_space=pl.ANY)],
            out_specs=pl.BlockSpec((SEL_TILE, D_MODEL), lambda j, b, c: (j, 0)),
            scratch_shapes=[pltpu.VMEM((N_EXPERTS, SEL_TILE + ROW_ALIGN, D_MODEL), bf16),
                            pltpu.SemaphoreType.DMA((N_EXPERTS,))],
        ),
        out_shape=jax.ShapeDtypeStruct((t, D_MODEL), f32),
        compiler_params=_cparams(("arbitrary",)),
        name="combine",
    )(base, cnt, rk, aff, x2, gfin, ye)


def _prep_layer(l, norm_mix, w_in, conv_w, a_log, dt_bias, a_norm, w_proj_a, w_proj_b, w_out,
                norm_ffn, w_router, w_gate, w_up, w_down):
    w = w_in[l]
    a3 = 3 * A_WIDTH
    small0 = 4 * A_WIDTH
    small1 = small0 + 4 * A_HEADS
    qkvb1 = small1 + 3 * B_HEADS * HEAD_DIM
    w_main = jnp.concatenate([w[:, qkvb1:], w[:, :a3], w[:, a3:small0], w[:, small1:qkvb1]], axis=1).astype(bf16)
    w_small = jnp.pad(w[:, small0:small1], ((0, 0), (0, LANES - 4 * A_HEADS))).astype(bf16)
    par = jnp.zeros((8, LANES), f32)
    par = par.at[0, 2 * A_HEADS:4 * A_HEADS].set(jnp.exp(a_log[l].astype(f32)).reshape(-1))
    par = par.at[1, 2 * A_HEADS:4 * A_HEADS].set(dt_bias[l].astype(f32).reshape(-1))
    return dict(
        g_mix=norm_mix[l].reshape(1, D_MODEL), w_main=w_main, w_small=w_small, conv_w=conv_w[l], par=par,
        an=jnp.tile(a_norm[l], 2).reshape(1, LANES),
        wpa=w_proj_a[l].astype(bf16), wpb=w_proj_b[l].astype(bf16), wout=w_out[l].astype(bf16),
        g_ffn=norm_ffn[l].reshape(1, D_MODEL),
        wr=jnp.pad(w_router[l], ((0, 0), (0, LANES - N_EXPERTS))),
        wg=w_gate[l].astype(bf16), wu=w_up[l].astype(bf16), wd=w_down[l].astype(bf16),
    )


def _layer(x2, b, s, lw, gfin, final_norm):
    t = b * s
    main2, small2 = _inproj(x2, lw["g_mix"], lw["w_main"], lw["w_small"])
    main3 = main2.reshape(b, s, MAIN_WIDTH)
    qkvn = _convprep(main3, lw["conv_w"])
    oa = _delta(qkvn, main3, small2.reshape(b, s, LANES), lw["par"], lw["an"])
    obs, lses = [], []
    for g in range(len(B_GROUPS)):
        o_g, l_g = _attention(main3, g)
        obs.append(o_g)
        lses.append(l_g)
    x2 = _merge(x2, oa.reshape(t, A_WIDTH), obs, lses, main2, lw["wpa"], lw["wpb"], lw["wout"])

    cap = (CAPACITY_FACTOR * t) // N_EXPERTS
    hn, aff, afft = _router(x2, lw["g_ffn"], lw["wr"])
    thr, need = _threshold(afft, cap)
    pad_row = lambda a: jnp.pad(a[:, 0], (0, LANES - N_EXPERTS)).reshape(1, LANES)
    rk, tb = _select(aff, pad_row(thr), pad_row(need))
    base = tb[:, 0, :N_EXPERTS].reshape(-1)
    cnt = tb[:, 1, :N_EXPERTS].reshape(-1)
    xe = _dispatch(base, cnt, rk, hn, cap)
    ye = _ffn(xe, lw["wg"], lw["wu"], lw["wd"], cap)
    return _combine(base, cnt, rk, aff, x2, gfin, ye, cap, final_norm)


def kernel(x_prompt, x_sample, norm_mix, w_in, conv_w, a_log, dt_bias, a_norm, w_proj_a, w_proj_b, w_out,
           norm_ffn, w_router, w_gate, w_up, w_down, norm_final):
    depth = w_in.shape[0]
    layers = [_prep_layer(l, norm_mix, w_in, conv_w, a_log, dt_bias, a_norm, w_proj_a, w_proj_b, w_out,
                          norm_ffn, w_router, w_gate, w_up, w_down) for l in range(depth)]
    gfin = norm_final.reshape(1, D_MODEL)
    outs = []
    for x in (x_prompt, x_sample):
        b, s, d = x.shape
        x2 = x.reshape(b * s, d)
        for l in range(depth):
            x2 = _layer(x2, b, s, layers[l], gfin, l == depth - 1)
        outs.append(x2.reshape(b, s, d))
    return tuple(outs)
```

```python
import functools

import numpy as np
import jax
import jax.numpy as jnp
from jax import lax
from jax.experimental import pallas as pl
from jax.experimental.pallas import tpu as pltpu

f32 = jnp.float32
bf16 = jnp.bfloat16
i32 = jnp.int32
HIGHEST = lax.Precision.HIGHEST

D_MODEL = 1024
A_HEADS = 8
HEAD_DIM = 64
A_WIDTH = A_HEADS * HEAD_DIM
A_CONV = 5
CHUNK = 64
B_GROUPS = ((128, 1), (512, 4), (2048, 16))
B_HEADS_PER_GROUP = 4
B_HEADS = B_HEADS_PER_GROUP * len(B_GROUPS)
B_GROUP_WIDTH = B_HEADS_PER_GROUP * HEAD_DIM
B_SIDE = 64
N_EXPERTS = 16
D_EXPERT = 1024
CAPACITY_FACTOR = 2
EPS = 1e-6
NEG = -1e30

LANES = 128
MAIN_WIDTH = 2 * D_MODEL + 4 * A_WIDTH + 3 * B_HEADS * HEAD_DIM
QKV_A_BLK = 2 * D_MODEL // LANES
Z_BLK = QKV_A_BLK + 3 * A_WIDTH // LANES
QKV_B_BLK256 = (2 * D_MODEL + 4 * A_WIDTH) // B_GROUP_WIDTH
MAIN_BLK256 = MAIN_WIDTH // B_GROUP_WIDTH
N_TILE = 1280
SEL_TILE = 256
SMALL_BUCKET = 64
ROW_ALIGN = 16
VMEM_LIMIT = 56 * 1024 * 1024


def _cparams(sem):
    return pltpu.CompilerParams(dimension_semantics=sem, vmem_limit_bytes=VMEM_LIMIT)


def _sigmoid(x):
    return 1.0 / (1.0 + jnp.exp(-x))


def _softplus(x):
    return jnp.maximum(x, 0.0) + jnp.log(1.0 + jnp.exp(-jnp.abs(x)))


def _inproj_kernel(x_ref, g_ref, w_ref, ws_ref, o_ref, os_ref, n_ref):
    @pl.when(pl.program_id(1) == 0)
    def _():
        x = x_ref[...]
        ms = jnp.mean(x * x, axis=-1, keepdims=True)
        n = (x * lax.rsqrt(ms + EPS) * g_ref[...]).astype(bf16)
        n_ref[...] = n
        os_ref[...] = jnp.dot(n, ws_ref[...], preferred_element_type=f32)

    o_ref[...] = jnp.dot(n_ref[...], w_ref[...], preferred_element_type=f32).astype(o_ref.dtype)


def _inproj(x2, g, w_main, w_small):
    t = x2.shape[0]
    tm = min(1024, t)
    return pl.pallas_call(
        _inproj_kernel,
        grid=(t // tm, MAIN_WIDTH // N_TILE),
        in_specs=[
            pl.BlockSpec((tm, D_MODEL), lambda i, j: (i, 0)),
            pl.BlockSpec((1, D_MODEL), lambda i, j: (0, 0)),
            pl.BlockSpec((D_MODEL, N_TILE), lambda i, j: (0, j)),
            pl.BlockSpec((D_MODEL, LANES), lambda i, j: (0, 0)),
        ],
        out_specs=[
            pl.BlockSpec((tm, N_TILE), lambda i, j: (i, j)),
            pl.BlockSpec((tm, LANES), lambda i, j: (i, 0)),
        ],
        out_shape=[jax.ShapeDtypeStruct((t, MAIN_WIDTH), bf16), jax.ShapeDtypeStruct((t, LANES), f32)],
        scratch_shapes=[pltpu.VMEM((tm, D_MODEL), bf16)],
        compiler_params=_cparams(("parallel", "arbitrary")),
        name="inproj",
    )(x2, g, w_main, w_small)


HALO = 16


def _convprep_kernel(cur_ref, prev_ref, next_ref, w_ref, o_ref, a_ref, *, ts, n_tiles):
    i = pl.program_id(1)
    c = pl.program_id(2)
    a_ref[0:HALO, :] = prev_ref[0].astype(f32) * (i > 0).astype(f32)
    a_ref[HALO:HALO + ts, :] = cur_ref[0].astype(f32)
    a_ref[HALO + ts:2 * HALO + ts, :] = next_ref[0].astype(f32) * (i < n_tiles - 1).astype(f32)
    w = w_ref[...]
    pad = (A_CONV - 1) // 2
    y = jnp.zeros((ts, LANES), f32)
    for j in range(A_CONV):
        off = HALO - pad + j
        y = y + a_ref[off:off + ts, :] * w[j:j + 1]
    y = y * _sigmoid(y)
    is_qk = c < 2 * A_WIDTH // LANES

    @pl.when(is_qk)
    def _():
        r = lax.broadcasted_iota(i32, (LANES, LANES), 0) // HEAD_DIM
        cc = lax.broadcasted_iota(i32, (LANES, LANES), 1) // HEAD_DIM
        head_ones = (r == cc).astype(f32)
        ss = jnp.dot(y * y, head_ones, preferred_element_type=f32, precision=HIGHEST)
        qscale = jnp.where(c < A_WIDTH // LANES, HEAD_DIM ** -0.5, 1.0).astype(f32)
        o_ref[0] = (y * (lax.rsqrt(ss + EPS) * qscale)).astype(o_ref.dtype)

    @pl.when(jnp.logical_not(is_qk))
    def _():
        o_ref[0] = y.astype(o_ref.dtype)


def _convprep(main3, conv_w):
    b, s, _ = main3.shape
    ts = min(1024, s)
    n_tiles = s // ts
    hb = ts // HALO
    nblk = 3 * A_WIDTH // LANES
    return pl.pallas_call(
        functools.partial(_convprep_kernel, ts=ts, n_tiles=n_tiles),
        grid=(b, n_tiles, nblk),
        in_specs=[
            pl.BlockSpec((1, ts, LANES), lambda bi, i, c: (bi, i, QKV_A_BLK + c)),
            pl.BlockSpec((1, HALO, LANES), lambda bi, i, c: (bi, jnp.maximum(i * hb - 1, 0), QKV_A_BLK + c)),
            pl.BlockSpec((1, HALO, LANES),
                         lambda bi, i, c: (bi, jnp.minimum((i + 1) * hb, s // HALO - 1), QKV_A_BLK + c)),
            pl.BlockSpec((A_CONV, LANES), lambda bi, i, c: (0, c)),
        ],
        out_specs=pl.BlockSpec((1, ts, LANES), lambda bi, i, c: (bi, i, c)),
        out_shape=jax.ShapeDtypeStruct((b, s, 3 * A_WIDTH), bf16),
        scratch_shapes=[pltpu.VMEM((ts + 2 * HALO, LANES), f32)],
        compiler_params=_cparams(("parallel", "parallel", "parallel")),
        name="convprep",
    )(main3, main3, main3, conv_w)


GROUP = 4


def _split_heads(x, mask):
    z = jnp.zeros_like(x)
    return jnp.concatenate([jnp.where(mask, x, z), jnp.where(mask, z, x)], axis=0)


def _delta_stages(pair):
    c = CHUNK
    lane = lax.broadcasted_iota(i32, (c, LANES), 1)
    m0 = lane < HEAD_DIM
    m0w = jnp.concatenate([m0, m0], axis=1)
    rl = lax.broadcasted_iota(i32, (c, LANES), 0)
    cl = lane % HEAD_DIM
    ri = lax.broadcasted_iota(i32, (c, c), 0)
    ci = lax.broadcasted_iota(i32, (c, c), 1)
    eye_p = (rl == cl).astype(f32)
    same_head = (lax.broadcasted_iota(i32, (LANES, LANES), 0) // HEAD_DIM
                 == lax.broadcasted_iota(i32, (LANES, LANES), 1) // HEAD_DIM)
    bd = lambda x: _split_heads(x, m0).astype(bf16)
    bdw = lambda x: _split_heads(x, m0w).astype(bf16)

    def pick(full, base):
        c0 = jnp.sum(jnp.where(lane == base + 2 * pair, full, 0.0), axis=1, keepdims=True)
        c1 = jnp.sum(jnp.where(lane == base + 2 * pair + 1, full, 0.0), axis=1, keepdims=True)
        return jnp.where(m0, c0, c1)

    def gates(st):
        d = 1 if st["upper"] else 0
        g_p = pick(st["gfull"], 2 * A_HEADS + d * A_HEADS)
        st["beta"] = pick(st["bfull"], d * A_HEADS)
        cum_mat = ((ri <= ci) if st["upper"] else (ri >= ci)).astype(f32)
        st["cum"] = jnp.dot(cum_mat, g_p, preferred_element_type=f32, precision=HIGHEST)

    def gram(st):
        q, k = st["q"], st["k"]
        st["gq"] = lax.dot_general(jnp.concatenate([q, k], axis=0), _split_heads(k, m0), (((1,), (1,)), ((), ())),
                                   preferred_element_type=f32)

    def decay(st):
        cum = st["cum"]
        cum_t = cum.T
        cum_row = jnp.concatenate([cum_t[0:1], cum_t[HEAD_DIM:HEAD_DIM + 1]], axis=1)
        incl = (rl <= cl) if st["upper"] else (rl >= cl)
        strict = (rl < cl) if st["upper"] else (rl > cl)
        dec = jnp.where(incl, jnp.exp(jnp.minimum(cum - cum_row, 0.0)), 0.0)
        gq = st.pop("gq")
        st["attn"] = (gq[0:c] * dec).astype(bf16)
        a = jnp.where(strict, -(st["beta"] * gq[c:2 * c] * dec), 0.0)
        st["p"] = eye_p + a
        st["a"] = jnp.dot(a.astype(bf16), bd(a), preferred_element_type=f32)

    def double(st):
        a, p = st["a"], st["p"]
        y = jnp.dot(jnp.concatenate([a, p], axis=0).astype(bf16), bd(a), preferred_element_type=f32)
        st["a"] = y[0:c]
        st["p"] = p + y[c:2 * c]

    def solve(st):
        a, p = st.pop("a"), st.pop("p")
        tinv = (p + jnp.dot(p.astype(bf16), bd(a), preferred_element_type=f32)).astype(bf16)
        cum = st["cum"]
        st["gtot"] = cum[0:1] if st["upper"] else cum[c - 1:c]
        st["eg"] = jnp.exp(cum)
        kf = st["k"].astype(f32)
        rhs = jnp.concatenate([st["v"].astype(f32) * st["beta"], kf * st["beta"] * st["eg"]], axis=1)
        st["uw"] = jnp.dot(tinv, bdw(rhs), preferred_element_type=f32)

    def finish(st):
        uw = st.pop("uw")
        aw = jnp.dot(st.pop("attn"), bdw(uw), preferred_element_type=f32)
        kd = (st["k"].astype(f32) * jnp.exp(st["gtot"] - st["cum"])).astype(bf16)
        t = lax.dot_general(kd, uw.astype(bf16), (((0,), (0,)), ((), ())), preferred_element_type=f32)
        st["oin"] = aw[:, 0:LANES]
        st["qeff"] = (st["q"].astype(f32) * st["eg"] - aw[:, LANES:2 * LANES]).astype(bf16)
        st["bbd"] = jnp.where(same_head, t[:, 0:LANES], 0.0)
        st["abd"] = jnp.where(same_head, -t[:, LANES:2 * LANES], 0.0).astype(bf16)
        st["dec"] = jnp.exp(st["gtot"])

    n_double = int(np.log2(c)) - 2
    return [gates, gram, decay] + [double] * n_double + [solve, finish]


def _delta_kernel(q_ref, k_ref, v_ref, z_ref, sm_ref, par_ref, an_ref, o_ref,
                  acc_ref, st_ref, qe_ref, oi_ref, ab_ref, bb_ref, dc_ref, *, s):
    c = CHUNK
    n = s // c
    g = min(GROUP, n)
    ng = n // g
    pair = pl.program_id(1)
    acc_ref[...] = jnp.zeros_like(acc_ref)
    st_ref[...] = jnp.zeros_like(st_ref)
    a_row = par_ref[0:1, :]
    dt_row = par_ref[1:2, :]
    stages = _delta_stages(pair)

    def row_start(gi, t, d):
        cidx = gi * g + t if d == 0 else n - 1 - (gi * g + t)
        return pl.multiple_of(cidx * c, c)

    def group_step(gi_a, slot_a, gi_b, slot_b):
        streams = []
        if gi_a is not None:
            for t in range(g):
                for d in range(2):
                    r0 = row_start(gi_a, t, d)
                    sm = sm_ref[0, pl.ds(r0, c), :]
                    streams.append(dict(
                        q=q_ref[0, pl.ds(r0, c), :], k=k_ref[0, pl.ds(r0, c), :], v=v_ref[0, pl.ds(r0, c), :],
                        gfull=-a_row * _softplus(sm + dt_row), bfull=_sigmoid(sm), upper=(d == 1),
                        idx=(d * 2 + slot_a) * g + t))
        states = [st_ref[0], st_ref[1]] if gi_b is not None else None

        def recurrence(t):
            for d in range(2):
                r0 = row_start(gi_b, t, d)
                idx = (d * 2 + slot_b) * g + t
                y = jnp.dot(jnp.concatenate([ab_ref[idx], qe_ref[idx]], axis=0), states[d].astype(bf16),
                            preferred_element_type=f32)
                acc_ref[pl.ds(r0, c), :] += y[LANES:LANES + c] + oi_ref[idx]
                states[d] = states[d] * dc_ref[idx][0:1] + y[0:LANES] + bb_ref[idx]

        done_b = 0
        for si, stage in enumerate(stages):
            for st in streams:
                stage(st)
            if gi_b is not None and si >= 1 and done_b < g:
                recurrence(done_b)
                done_b += 1
        if gi_b is not None:
            for t in range(done_b, g):
                recurrence(t)
            st_ref[0] = states[0]
            st_ref[1] = states[1]
        for st in streams:
            idx = st["idx"]
            qe_ref[idx] = st["qeff"]
            oi_ref[idx] = st["oin"]
            ab_ref[idx] = st["abd"]
            bb_ref[idx] = st["bbd"]
            dc_ref[idx] = jnp.broadcast_to(st["dec"], (8, LANES))

    group_step(0, 0, None, None)

    def body(i, carry):
        group_step(i, i % 2, i - 1, (i - 1) % 2)
        return carry

    lax.fori_loop(1, ng, body, 0)
    group_step(None, None, ng - 1, (ng - 1) % 2)

    rows = min(256, s)
    r = lax.broadcasted_iota(i32, (LANES, LANES), 0) // HEAD_DIM
    cc = lax.broadcasted_iota(i32, (LANES, LANES), 1) // HEAD_DIM
    head_mean = (r == cc).astype(f32) * (1.0 / HEAD_DIM)

    def epi(i, carry):
        r0 = pl.multiple_of(i * rows, rows)
        o = acc_ref[pl.ds(r0, rows), :]
        ms = jnp.dot(o * o, head_mean, preferred_element_type=f32, precision=HIGHEST)
        z = z_ref[0, pl.ds(r0, rows), :].astype(f32)
        y = o * lax.rsqrt(ms + EPS) * an_ref[...] * (z * _sigmoid(z))
        o_ref[0, pl.ds(r0, rows), :] = y.astype(o_ref.dtype)
        return carry

    lax.fori_loop(0, s // rows, epi, 0)


def _delta(qkvn, main3, small3, par, an):
    b, s, _ = qkvn.shape
    npair = A_HEADS // 2
    kb = A_WIDTH // LANES
    nbuf = 4 * min(GROUP, s // CHUNK)
    seq = lambda off: pl.BlockSpec((1, s, LANES), lambda bi, p: (bi, 0, off + p))
    return pl.pallas_call(
        functools.partial(_delta_kernel, s=s),
        grid=(b, npair),
        in_specs=[
            seq(0), seq(kb), seq(2 * kb), seq(Z_BLK),
            pl.BlockSpec((1, s, LANES), lambda bi, p: (bi, 0, 0)),
            pl.BlockSpec((8, LANES), lambda bi, p: (0, 0)),
            pl.BlockSpec((1, LANES), lambda bi, p: (0, 0)),
        ],
        out_specs=pl.BlockSpec((1, s, LANES), lambda bi, p: (bi, 0, p)),
        out_shape=jax.ShapeDtypeStruct((b, s, A_WIDTH), bf16),
        scratch_shapes=[pltpu.VMEM((s, LANES), f32), pltpu.VMEM((2, LANES, LANES), f32),
                        pltpu.VMEM((nbuf, CHUNK, LANES), bf16), pltpu.VMEM((nbuf, CHUNK, LANES), f32),
                        pltpu.VMEM((nbuf, LANES, LANES), bf16), pltpu.VMEM((nbuf, LANES, LANES), f32),
                        pltpu.VMEM((nbuf, 8, LANES), f32)],
        compiler_params=_cparams(("parallel", "arbitrary")),
        name="delta",
    )(qkvn, qkvn, qkvn, main3, small3, par, an)


def _attn_kernel(q_ref, k_ref, v_ref, o_ref, l_ref, *, lp, bq, nk, dil, slopes):
    w = B_GROUP_WIDTH
    head_of_lane = lax.broadcasted_iota(i32, (1, w), 1) // HEAD_DIM
    scale = HEAD_DIM ** -0.5

    def body(i, carry):
        q0 = pl.multiple_of(i * bq, bq)
        ks = pl.multiple_of(jnp.clip(q0 - B_SIDE, 0, lp - nk), B_SIDE)
        q = q_ref[0, pl.ds(q0, bq), :]
        k = k_ref[0, pl.ds(ks, nk), :]
        v = v_ref[0, pl.ds(ks, nk), :]
        zq = jnp.zeros_like(q)
        qs = jnp.concatenate([jnp.where(head_of_lane == h, q, zq) for h in range(B_HEADS_PER_GROUP)], axis=0)
        sc = lax.dot_general(qs, k, (((1,), (1,)), ((), ())), preferred_element_type=f32) * scale
        qpos = q0 + lax.broadcasted_iota(i32, (bq, nk), 0)
        kpos = ks + lax.broadcasted_iota(i32, (bq, nk), 1)
        adelta = jnp.abs(kpos - qpos)
        valid = adelta <= B_SIDE
        dist = adelta.astype(f32) * float(dil)
        o = jnp.zeros((bq, w), f32)
        lse = jnp.zeros((bq, w), f32)
        for h in range(B_HEADS_PER_GROUP):
            sh = jnp.where(valid, sc[h * bq:(h + 1) * bq] - slopes[h] * dist, NEG)
            m = jnp.max(sh, axis=1, keepdims=True)
            p = jnp.exp(sh - m)
            l = jnp.sum(p, axis=1, keepdims=True)
            oh = jnp.dot(p.astype(bf16), v, preferred_element_type=f32) / l
            hm = head_of_lane == h
            o = jnp.where(hm, oh, o)
            lse = jnp.where(hm, m + jnp.log(l), lse)
        o_ref[0, pl.ds(q0, bq), :] = o.astype(o_ref.dtype)
        l_ref[0, pl.ds(q0, bq), :] = lse
        return carry

    lax.fori_loop(0, lp // bq, body, 0)


def _attention(main3, group):
    b, s, _ = main3.shape
    _, dil = B_GROUPS[group]
    lp = s // dil
    bq = min(128, lp)
    nk = min(bq + 2 * B_SIDE, lp)
    slopes = tuple(float(2.0 ** (-8.0 * (group * B_HEADS_PER_GROUP + h + 1) / B_HEADS))
                   for h in range(B_HEADS_PER_GROUP))
    w = B_GROUP_WIDTH
    ngrp = len(B_GROUPS)
    blk = lambda off: QKV_B_BLK256 + off * ngrp + group
    if dil == 1:
        src = main3
        spec = lambda off: pl.BlockSpec((1, lp, w), lambda bi, r: (bi, 0, blk(off)))
    else:
        qkv = jnp.concatenate([main3[:, :, blk(off) * w:(blk(off) + 1) * w] for off in range(3)], axis=-1)
        src = qkv.reshape(b, lp, dil * 3 * w)
        spec = lambda off: pl.BlockSpec((1, lp, w), lambda bi, r: (bi, 0, r * 3 + off))
    o, lse = pl.pallas_call(
        functools.partial(_attn_kernel, lp=lp, bq=bq, nk=nk, dil=dil, slopes=slopes),
        grid=(b, dil),
        in_specs=[spec(0), spec(1), spec(2)],
        out_specs=[pl.BlockSpec((1, lp, w), lambda bi, r: (bi, 0, r)),
                   pl.BlockSpec((1, lp, w), lambda bi, r: (bi, 0, r))],
        out_shape=[jax.ShapeDtypeStruct((b, lp, dil * w), bf16), jax.ShapeDtypeStruct((b, lp, dil * w), f32)],
        compiler_params=_cparams(("parallel", "parallel")),
        name=f"attn_d{dil}",
    )(src, src, src)
    return o.reshape(b * s, w), lse.reshape(b * s, w)


def _merge_kernel(x_ref, oa_ref, o1_ref, o2_ref, o3_ref, l1_ref, l2_ref, l3_ref, ga_ref, gb_ref,
                  wpa_ref, wpb_ref, wout_ref, out_ref):
    l1, l2, l3 = l1_ref[...], l2_ref[...], l3_ref[...]
    m = jnp.maximum(jnp.maximum(l1, l2), l3)
    e1, e2, e3 = jnp.exp(l1 - m), jnp.exp(l2 - m), jnp.exp(l3 - m)
    ob = (e1 * o1_ref[...].astype(f32) + e2 * o2_ref[...].astype(f32) + e3 * o3_ref[...].astype(f32)) / (e1 + e2 + e3)
    ya = jnp.dot(oa_ref[...], wpa_ref[...], preferred_element_type=f32)
    yb = jnp.dot(ob.astype(bf16), wpb_ref[...], preferred_element_type=f32)
    mix = _sigmoid(ga_ref[...].astype(f32)) * ya + _sigmoid(gb_ref[...].astype(f32)) * yb
    out_ref[...] = x_ref[...] + jnp.dot(mix.astype(bf16), wout_ref[...], preferred_element_type=f32)


def _merge(x2, oa2, obs, lses, main2, wpa, wpb, wout):
    t = x2.shape[0]
    tm = min(512, t)
    w = B_GROUP_WIDTH
    row = lambda width: pl.BlockSpec((tm, width), lambda i: (i, 0))
    full = lambda a, bb: pl.BlockSpec((a, bb), lambda i: (0, 0))
    return pl.pallas_call(
        _merge_kernel,
        grid=(t // tm,),
        in_specs=[row(D_MODEL), row(A_WIDTH), row(w), row(w), row(w), row(w), row(w), row(w),
                  pl.BlockSpec((tm, D_MODEL), lambda i: (i, 0)), pl.BlockSpec((tm, D_MODEL), lambda i: (i, 1)),
                  full(A_WIDTH, D_MODEL), full(w, D_MODEL), full(D_MODEL, D_MODEL)],
        out_specs=row(D_MODEL),
        out_shape=jax.ShapeDtypeStruct((t, D_MODEL), f32),
        compiler_params=_cparams(("parallel",)),
        name="merge",
    )(x2, oa2, *obs, *lses, main2, main2, wpa, wpb, wout)


def _router_kernel(x_ref, g_ref, wr_ref, hn_ref, aff_ref, afft_ref):
    x = x_ref[...]
    ms = jnp.mean(x * x, axis=-1, keepdims=True)
    hn = x * lax.rsqrt(ms + EPS) * g_ref[...]
    hn_ref[...] = hn.astype(bf16)
    logits = jnp.dot(hn, wr_ref[...], preferred_element_type=f32, precision=HIGHEST)
    lane = lax.broadcasted_iota(i32, logits.shape, 1)
    logits = jnp.where(lane < N_EXPERTS, logits, NEG)
    m = jnp.max(logits, axis=1, keepdims=True)
    e = jnp.exp(logits - m)
    aff = e / jnp.sum(e, axis=1, keepdims=True)
    aff_ref[...] = aff
    afft_ref[...] = aff.T[0:N_EXPERTS]


def _router(x2, g, wr):
    t = x2.shape[0]
    tm = min(512, t)
    return pl.pallas_call(
        _router_kernel,
        grid=(t // tm,),
        in_specs=[pl.BlockSpec((tm, D_MODEL), lambda i: (i, 0)),
                  pl.BlockSpec((1, D_MODEL), lambda i: (0, 0)),
                  pl.BlockSpec((D_MODEL, LANES), lambda i: (0, 0))],
        out_specs=[pl.BlockSpec((tm, D_MODEL), lambda i: (i, 0)),
                   pl.BlockSpec((tm, LANES), lambda i: (i, 0)),
                   pl.BlockSpec((N_EXPERTS, tm), lambda i: (0, i))],
        out_shape=[jax.ShapeDtypeStruct((t, D_MODEL), bf16), jax.ShapeDtypeStruct((t, LANES), f32),
                   jax.ShapeDtypeStruct((N_EXPERTS, t), f32)],
        compiler_params=_cparams(("parallel",)),
        name="router",
    )(x2, g, wr)


def _threshold_kernel(afft_ref, thr_ref, need_ref, *, cap):
    bits = lax.bitcast_convert_type(afft_ref[...], i32)

    def step(i, lo):
        cand = lo | lax.shift_left(jnp.int32(1), 30 - i)
        cnt = jnp.sum((bits >= cand).astype(i32), axis=1, keepdims=True)
        return jnp.where(cnt >= cap, cand, lo)

    thr = lax.fori_loop(0, 31, step, jnp.zeros((N_EXPERTS, 1), i32))
    n_gt = jnp.sum((bits > thr).astype(i32), axis=1, keepdims=True)
    thr_ref[...] = jnp.broadcast_to(thr, (N_EXPERTS, LANES))
    need_ref[...] = jnp.broadcast_to(cap - n_gt, (N_EXPERTS, LANES))


def _threshold(afft, cap):
    return pl.pallas_call(
        functools.partial(_threshold_kernel, cap=cap),
        out_shape=[jax.ShapeDtypeStruct((N_EXPERTS, LANES), i32), jax.ShapeDtypeStruct((N_EXPERTS, LANES), i32)],
        compiler_params=pltpu.CompilerParams(vmem_limit_bytes=VMEM_LIMIT),
        name="threshold",
    )(afft)


def _select_kernel(aff_ref, thr_ref, need_ref, rk_ref, tb_ref, carry_ref):
    tt = SEL_TILE

    @pl.when(pl.program_id(0) == 0)
    def _():
        carry_ref[...] = jnp.zeros_like(carry_ref)

    bits = lax.bitcast_convert_type(aff_ref[...], i32)
    lane_ok = lax.broadcasted_iota(i32, (tt, LANES), 1) < N_EXPERTS
    thr = thr_ref[...]
    gt = (bits > thr) & lane_ok
    eq = (bits == thr) & lane_ok
    below = (lax.broadcasted_iota(i32, (tt, tt), 0) > lax.broadcasted_iota(i32, (tt, tt), 1)).astype(bf16)
    tie_carry = carry_ref[0:1, :]
    base = carry_ref[1:2, :]
    eqf = eq.astype(f32)
    tie_before = jnp.dot(below, eq.astype(bf16), preferred_element_type=f32) + tie_carry
    sel = gt | (eq & (tie_before < need_ref[...].astype(f32)))
    self_ = sel.astype(f32)
    rank = jnp.dot(below, sel.astype(bf16), preferred_element_type=f32)
    n = jnp.sum(self_, axis=0, keepdims=True)
    rk_ref[...] = jnp.where(sel, rank, -1.0)
    tb_ref[0, 0:1, :] = base.astype(i32)
    tb_ref[0, 1:2, :] = n.astype(i32)
    tb_ref[0, 2:8, :] = jnp.zeros((6, LANES), i32)
    carry_ref[0:1, :] = tie_carry + jnp.sum(eqf, axis=0, keepdims=True)
    carry_ref[1:2, :] = base + n


def _select(aff, thr_row, need_row):
    t = aff.shape[0]
    nt = t // SEL_TILE
    return pl.pallas_call(
        _select_kernel,
        grid=(nt,),
        in_specs=[pl.BlockSpec((SEL_TILE, LANES), lambda j: (j, 0)),
                  pl.BlockSpec((1, LANES), lambda j: (0, 0)),
                  pl.BlockSpec((1, LANES), lambda j: (0, 0))],
        out_specs=[pl.BlockSpec((SEL_TILE, LANES), lambda j: (j, 0)),
                   pl.BlockSpec((1, 8, LANES), lambda j: (j, 0, 0))],
        out_shape=[jax.ShapeDtypeStruct((t, LANES), f32), jax.ShapeDtypeStruct((nt, 8, LANES), i32)],
        scratch_shapes=[pltpu.VMEM((8, LANES), f32)],
        compiler_params=_cparams(("arbitrary",)),
        name="select",
    )(aff, thr_row, need_row)


WIN = SMALL_BUCKET + ROW_ALIGN
BIG_WIN = SEL_TILE + ROW_ALIGN


def _buckets(n):
    return ((WIN, (n > 0) & (n <= SMALL_BUCKET)), (BIG_WIN, n > SMALL_BUCKET))


def _aligned(x):
    return pl.multiple_of(x - x % ROW_ALIGN, ROW_ALIGN)


def _dispatch_kernel(base_ref, cnt_ref, fast_ref, rk_ref, hn_ref, xe_ref,
                     stage_ref, big_ref, carry_ref, sem_ref, bsem_ref):
    j = pl.program_id(0)
    nt = pl.num_programs(0)
    tt = SEL_TILE
    ne = N_EXPERTS
    slot = j % 2

    @pl.when(j == 0)
    def _():
        carry_ref[...] = jnp.zeros_like(carry_ref)

    rkt = rk_ref[...].T
    hn = hn_ref[...]

    def targets(e):
        rem = base_ref[j * ne + e] % ROW_ALIGN
        row = rkt[e:e + 1, :]
        return jnp.where(row >= 0.0, row + rem.astype(f32), -1.0)

    def merge_carry(ref_rows, e):
        n = cnt_ref[j * ne + e]
        rem = base_ref[j * ne + e] % ROW_ALIGN
        head = ref_rows(0, ROW_ALIGN)
        head[...] += carry_ref[e]
        keep = pl.multiple_of(((rem + n) // ROW_ALIGN) * ROW_ALIGN, ROW_ALIGN)
        carry_ref[e] = ref_rows(keep, ROW_ALIGN)[...]

    def fast_copy(jj, sl, e):
        return pltpu.make_async_copy(stage_ref.at[sl, pl.ds(e * WIN, WIN)],
                                     xe_ref.at[e, pl.ds(_aligned(base_ref[jj * ne + e]), WIN)], sem_ref.at[e])

    def wait_previous():
        jp = jnp.maximum(j - 1, 0)

        @pl.when((j > 0) & (fast_ref[jp] == 1))
        def _():
            for e in range(ne):
                @pl.when(cnt_ref[jp * ne + e] > 0)
                def _(e=e):
                    fast_copy(jp, 1 - slot, e).wait()

    @pl.when(fast_ref[j] == 1)
    def _():
        win_slot = lax.broadcasted_iota(i32, (WIN, tt), 0).astype(f32)
        onehot = jnp.concatenate([(win_slot == targets(e)).astype(bf16) for e in range(ne)], axis=0)
        stage_ref[slot] = jnp.dot(onehot, hn, preferred_element_type=f32).astype(bf16)
        for e in range(ne):
            @pl.when(cnt_ref[j * ne + e] > 0)
            def _(e=e):
                merge_carry(lambda st, sz: stage_ref.at[slot, pl.ds(e * WIN + st, sz)], e)
        wait_previous()
        for e in range(ne):
            @pl.when(cnt_ref[j * ne + e] > 0)
            def _(e=e):
                fast_copy(j, slot, e).start()

        @pl.when(j == nt - 1)
        def _():
            for e in range(ne):
                @pl.when(cnt_ref[j * ne + e] > 0)
                def _(e=e):
                    fast_copy(j, slot, e).wait()

    @pl.when(fast_ref[j] == 0)
    def _():
        wait_previous()
        big_copy = lambda e, rows: pltpu.make_async_copy(
            big_ref.at[e, pl.ds(0, rows)], xe_ref.at[e, pl.ds(_aligned(base_ref[j * ne + e]), rows)], bsem_ref.at[e])
        for e in range(ne):
            for rows, cond in _buckets(cnt_ref[j * ne + e]):
                @pl.when(cond)
                def _(rows=rows, e=e):
                    win_slot = lax.broadcasted_iota(i32, (rows, tt), 0).astype(f32)
                    onehot = (win_slot == targets(e)).astype(bf16)
                    big_ref[e, 0:rows, :] = jnp.dot(onehot, hn, preferred_element_type=f32).astype(bf16)
                    merge_carry(lambda st, sz: big_ref.at[e, pl.ds(st, sz)], e)
                    big_copy(e, rows).start()
        for e in range(ne):
            for rows, cond in _buckets(cnt_ref[j * ne + e]):
                @pl.when(cond)
                def _(rows=rows, e=e):
                    big_copy(e, rows).wait()


def _dispatch(base, cnt, fast, rk, hn, cap):
    t = hn.shape[0]
    nt = t // SEL_TILE
    return pl.pallas_call(
        _dispatch_kernel,
        grid_spec=pltpu.PrefetchScalarGridSpec(
            num_scalar_prefetch=3,
            grid=(nt,),
            in_specs=[pl.BlockSpec((SEL_TILE, LANES), lambda j, b, c, f: (j, 0)),
                      pl.BlockSpec((SEL_TILE, D_MODEL), lambda j, b, c, f: (j, 0))],
            out_specs=pl.BlockSpec(memory_space=pl.ANY),
            scratch_shapes=[pltpu.VMEM((2, N_EXPERTS * WIN, D_MODEL), bf16),
                            pltpu.VMEM((N_EXPERTS, BIG_WIN, D_MODEL), bf16),
                            pltpu.VMEM((N_EXPERTS, ROW_ALIGN, D_MODEL), bf16),
                            pltpu.SemaphoreType.DMA((N_EXPERTS,)),
                            pltpu.SemaphoreType.DMA((N_EXPERTS,))],
        ),
        out_shape=jax.ShapeDtypeStruct((N_EXPERTS, cap + BIG_WIN, D_MODEL), bf16),
        compiler_params=_cparams(("arbitrary",)),
        name="dispatch",
    )(base, cnt, fast, rk, hn)


def _ffn_kernel(x_ref, wg_ref, wu_ref, wd_ref, y_ref):
    x = x_ref[0]
    g = jnp.dot(x, wg_ref[0], preferred_element_type=f32)
    u = jnp.dot(x, wu_ref[0], preferred_element_type=f32)
    h = (g * _sigmoid(g) * u).astype(bf16)
    y_ref[0] = jnp.dot(h, wd_ref[0], preferred_element_type=f32).astype(y_ref.dtype)


def _ffn(xe, wg, wu, wd, cap):
    tr = min(512, cap)
    wspec = lambda a, bb: pl.BlockSpec((1, a, bb), lambda e, i: (e, 0, 0))
    return pl.pallas_call(
        _ffn_kernel,
        grid=(N_EXPERTS, cap // tr),
        in_specs=[pl.BlockSpec((1, tr, D_MODEL), lambda e, i: (e, i, 0)),
                  wspec(D_MODEL, D_EXPERT), wspec(D_MODEL, D_EXPERT), wspec(D_EXPERT, D_MODEL)],
        out_specs=pl.BlockSpec((1, tr, D_MODEL), lambda e, i: (e, i, 0)),
        out_shape=jax.ShapeDtypeStruct((N_EXPERTS, cap, D_MODEL), bf16),
        compiler_params=_cparams(("parallel", "parallel")),
        name="expert_ffn",
    )(xe, wg, wu, wd)


def _combine_kernel(base_ref, cnt_ref, fast_ref, rk_ref, aff_ref, x_ref, gf_ref, ye_ref, out_ref,
                    buf_ref, big_ref, sem_ref, bsem_ref, *, cap, final_norm):
    j = pl.program_id(0)
    nt = pl.num_programs(0)
    tt = SEL_TILE
    ne = N_EXPERTS
    slot = j % 2
    win = min(WIN, cap)
    window_start = lambda base, rows: pl.multiple_of(jnp.minimum(base - base % ROW_ALIGN, cap - rows), ROW_ALIGN)

    def win_copy(jj, sl, e):
        return pltpu.make_async_copy(ye_ref.at[e, pl.ds(window_start(base_ref[jj * ne + e], win), win)],
                                     buf_ref.at[sl, pl.ds(e * win, win)], sem_ref.at[sl])

    @pl.when((j == 0) & (fast_ref[0] == 1))
    def _():
        for e in range(ne):
            win_copy(0, 0, e).start()

    jn = jnp.minimum(j + 1, nt - 1)

    @pl.when((j + 1 < nt) & (fast_ref[jn] == 1))
    def _():
        for e in range(ne):
            win_copy(jn, 1 - slot, e).start()

    @pl.when(fast_ref[j] == 1)
    def _():
        rkt = rk_ref[...].T
        afft = aff_ref[...].T
        win_slot = lax.broadcasted_iota(i32, (win, tt), 0).astype(f32)
        his, los = [], []
        for e in range(ne):
            base = base_ref[j * ne + e]
            shift = (base - window_start(base, win)).astype(f32)
            row = rkt[e:e + 1, :]
            gate = jnp.where((win_slot == row + shift) & (row >= 0.0), afft[e:e + 1, :], 0.0)
            hi = gate.astype(bf16)
            his.append(hi)
            los.append((gate - hi.astype(f32)).astype(bf16))
        lhs = jnp.concatenate([jnp.concatenate(his, axis=0), jnp.concatenate(los, axis=0)], axis=1)
        for e in range(ne):
            win_copy(j, slot, e).wait()
        y = lax.dot_general(lhs, buf_ref[slot], (((0,), (0,)), ((), ())), preferred_element_type=f32)
        out_ref[...] = x_ref[...] + y[0:tt] + y[tt:2 * tt]

    @pl.when(fast_ref[j] == 0)
    def _():
        buckets = lambda n: tuple((min(rows, cap), cond) for rows, cond in _buckets(n))
        big_copy = lambda e, rows: pltpu.make_async_copy(
            ye_ref.at[e, pl.ds(window_start(base_ref[j * ne + e], rows), rows)], big_ref.at[e, pl.ds(0, rows)],
            bsem_ref.at[e])
        for e in range(ne):
            for rows, cond in buckets(cnt_ref[j * ne + e]):
                @pl.when(cond)
                def _(rows=rows, e=e):
                    big_copy(e, rows).start()
        out_ref[...] = x_ref[...]
        for e in range(ne):
            for rows, cond in buckets(cnt_ref[j * ne + e]):
                @pl.when(cond)
                def _(rows=rows, e=e):
                    base = base_ref[j * ne + e]
                    big_copy(e, rows).wait()
                    col = rk_ref[:, e:e + 1]
                    tgt = col + (base - window_start(base, rows)).astype(f32)
                    win_slot = lax.broadcasted_iota(i32, (tt, rows), 1).astype(f32)
                    onehot = ((tgt == win_slot) & (col >= 0.0)).astype(bf16)
                    contrib = jnp.dot(onehot, big_ref[e, 0:rows, :], preferred_element_type=f32)
                    out_ref[...] += contrib * aff_ref[:, e:e + 1]

    if final_norm:
        y = out_ref[...]
        ms = jnp.mean(y * y, axis=-1, keepdims=True)
        out_ref[...] = y * lax.rsqrt(ms + EPS) * gf_ref[...]


def _combine(base, cnt, fast, rk, aff, x2, gfin, ye, cap, final_norm):
    t = x2.shape[0]
    nt = t // SEL_TILE
    tile = lambda width: pl.BlockSpec((SEL_TILE, width), lambda j, b, c, f: (j, 0))
    return pl.pallas_call(
        functools.partial(_combine_kernel, cap=cap, final_norm=final_norm),
        grid_spec=pltpu.PrefetchScalarGridSpec(
            num_scalar_prefetch=3,
            grid=(nt,),
            in_specs=[tile(LANES), tile(LANES), tile(D_MODEL),
                      pl.BlockSpec((1, D_MODEL), lambda j, b, c, f: (0, 0)),
                      pl.BlockSpec(memory_space=pl.ANY)],
            out_specs=tile(D_MODEL),
            scratch_shapes=[pltpu.VMEM((2, N_EXPERTS * min(WIN, cap), D_MODEL), bf16),
                            pltpu.VMEM((N_EXPERTS, min(BIG_WIN, cap), D_MODEL), bf16),
                            pltpu.SemaphoreType.DMA((2,)),
                            pltpu.SemaphoreType.DMA((N_EXPERTS,))],
        ),
        out_shape=jax.ShapeDtypeStruct((t, D_MODEL), f32),
        compiler_params=_cparams(("arbitrary",)),
        name="combine",
    )(base, cnt, fast, rk, aff, x2, gfin, ye)


def _prep_layer(l, norm_mix, w_in, conv_w, a_log, dt_bias, a_norm, w_proj_a, w_proj_b, w_out,
                norm_ffn, w_router, w_gate, w_up, w_down):
    w = w_in[l]
    a3 = 3 * A_WIDTH
    small0 = 4 * A_WIDTH
    small1 = small0 + 4 * A_HEADS
    qkvb1 = small1 + 3 * B_HEADS * HEAD_DIM
    w_main = jnp.concatenate([w[:, qkvb1:], w[:, :a3], w[:, a3:small0], w[:, small1:qkvb1]], axis=1).astype(bf16)
    w_small = jnp.pad(w[:, small0:small1], ((0, 0), (0, LANES - 4 * A_HEADS))).astype(bf16)
    par = jnp.zeros((8, LANES), f32)
    par = par.at[0, 2 * A_HEADS:4 * A_HEADS].set(jnp.exp(a_log[l].astype(f32)).reshape(-1))
    par = par.at[1, 2 * A_HEADS:4 * A_HEADS].set(dt_bias[l].astype(f32).reshape(-1))
    return dict(
        g_mix=norm_mix[l].reshape(1, D_MODEL), w_main=w_main, w_small=w_small, conv_w=conv_w[l], par=par,
        an=jnp.tile(a_norm[l], 2).reshape(1, LANES),
        wpa=w_proj_a[l].astype(bf16), wpb=w_proj_b[l].astype(bf16), wout=w_out[l].astype(bf16),
        g_ffn=norm_ffn[l].reshape(1, D_MODEL),
        wr=jnp.pad(w_router[l], ((0, 0), (0, LANES - N_EXPERTS))),
        wg=w_gate[l].astype(bf16), wu=w_up[l].astype(bf16), wd=w_down[l].astype(bf16),
    )


def _layer(x2, b, s, lw, gfin, final_norm):
    t = b * s
    main2, small2 = _inproj(x2, lw["g_mix"], lw["w_main"], lw["w_small"])
    main3 = main2.reshape(b, s, MAIN_WIDTH)
    qkvn = _convprep(main3, lw["conv_w"])
    oa = _delta(qkvn, main3, small2.reshape(b, s, LANES), lw["par"], lw["an"])
    obs, lses = [], []
    for g in range(len(B_GROUPS)):
        o_g, l_g = _attention(main3, g)
        obs.append(o_g)
        lses.append(l_g)
    x2 = _merge(x2, oa.reshape(t, A_WIDTH), obs, lses, main2, lw["wpa"], lw["wpb"], lw["wout"])

    cap = (CAPACITY_FACTOR * t) // N_EXPERTS
    hn, aff, afft = _router(x2, lw["g_ffn"], lw["wr"])
    thr, need = _threshold(afft, cap)
    pad_row = lambda a: jnp.pad(a[:, 0], (0, LANES - N_EXPERTS)).reshape(1, LANES)
    rk, tb = _select(aff, pad_row(thr), pad_row(need))
    base = tb[:, 0, :N_EXPERTS].reshape(-1)
    cnt = tb[:, 1, :N_EXPERTS].reshape(-1)
    fast = (jnp.max(tb[:, 1, :N_EXPERTS], axis=1) <= SMALL_BUCKET).astype(i32)
    xe = _dispatch(base, cnt, fast, rk, hn, cap)
    ye = _ffn(xe, lw["wg"], lw["wu"], lw["wd"], cap)
    return _combine(base, cnt, fast, rk, aff, x2, gfin, ye, cap, final_norm)


def kernel(x_prompt, x_sample, norm_mix, w_in, conv_w, a_log, dt_bias, a_norm, w_proj_a, w_proj_b, w_out,
           norm_ffn, w_router, w_gate, w_up, w_down, norm_final):
    depth = w_in.shape[0]
    layers = [_prep_layer(l, norm_mix, w_in, conv_w, a_log, dt_bias, a_norm, w_proj_a, w_proj_b, w_out,
                          norm_ffn, w_router, w_gate, w_up, w_down) for l in range(depth)]
    gfin = norm_final.reshape(1, D_MODEL)
    outs = []
    for x in (x_prompt, x_sample):
        b, s, d = x.shape
        x2 = x.reshape(b * s, d)
        for l in range(depth):
            x2 = _layer(x2, b, s, layers[l], gfin, l == depth - 1)
        outs.append(x2.reshape(b, s, d))
    return tuple(outs)
```

```python
import functools

import numpy as np
import jax
import jax.numpy as jnp
from jax import lax
from jax.experimental import pallas as pl
from jax.experimental.pallas import tpu as pltpu

f32 = jnp.float32
bf16 = jnp.bfloat16
i32 = jnp.int32
HIGHEST = lax.Precision.HIGHEST

D_MODEL = 1024
A_HEADS = 8
HEAD_DIM = 64
A_WIDTH = A_HEADS * HEAD_DIM
A_CONV = 5
CHUNK = 64
B_GROUPS = ((128, 1), (512, 4), (2048, 16))
B_HEADS_PER_GROUP = 4
B_HEADS = B_HEADS_PER_GROUP * len(B_GROUPS)
B_GROUP_WIDTH = B_HEADS_PER_GROUP * HEAD_DIM
B_SIDE = 64
N_EXPERTS = 16
D_EXPERT = 1024
CAPACITY_FACTOR = 2
EPS = 1e-6
NEG = -1e30

LANES = 128
MAIN_WIDTH = 2 * D_MODEL + 4 * A_WIDTH + 3 * B_HEADS * HEAD_DIM
Z_BLK = 3 * A_WIDTH // LANES
GATE_BLK = 4 * A_WIDTH // D_MODEL
QKV_B_BLK256 = (2 * D_MODEL + 4 * A_WIDTH) // B_GROUP_WIDTH
MAIN_BLK256 = MAIN_WIDTH // B_GROUP_WIDTH
N_TILE = 1280
SEL_TILE = 256
SMALL_BUCKET = 64
ROW_ALIGN = 16
VMEM_LIMIT = 56 * 1024 * 1024


def _cparams(sem):
    return pltpu.CompilerParams(dimension_semantics=sem, vmem_limit_bytes=VMEM_LIMIT)


def _sigmoid(x):
    return 1.0 / (1.0 + jnp.exp(-x))


def _softplus(x):
    return jnp.maximum(x, 0.0) + jnp.log(1.0 + jnp.exp(-jnp.abs(x)))


def _inproj_kernel(x_ref, g_ref, w_ref, ws_ref, o_ref, os_ref, n_ref):
    @pl.when(pl.program_id(1) == 0)
    def _():
        x = x_ref[...]
        ms = jnp.mean(x * x, axis=-1, keepdims=True)
        n = (x * lax.rsqrt(ms + EPS) * g_ref[...]).astype(bf16)
        n_ref[...] = n
        os_ref[...] = jnp.dot(n, ws_ref[...], preferred_element_type=f32)

    o_ref[...] = jnp.dot(n_ref[...], w_ref[...], preferred_element_type=f32).astype(o_ref.dtype)


def _inproj(x2, g, w_main, w_small):
    t = x2.shape[0]
    tm = min(1024, t)
    return pl.pallas_call(
        _inproj_kernel,
        grid=(t // tm, MAIN_WIDTH // N_TILE),
        in_specs=[
            pl.BlockSpec((tm, D_MODEL), lambda i, j: (i, 0)),
            pl.BlockSpec((1, D_MODEL), lambda i, j: (0, 0)),
            pl.BlockSpec((D_MODEL, N_TILE), lambda i, j: (0, j)),
            pl.BlockSpec((D_MODEL, LANES), lambda i, j: (0, 0)),
        ],
        out_specs=[
            pl.BlockSpec((tm, N_TILE), lambda i, j: (i, j)),
            pl.BlockSpec((tm, LANES), lambda i, j: (i, 0)),
        ],
        out_shape=[jax.ShapeDtypeStruct((t, MAIN_WIDTH), bf16), jax.ShapeDtypeStruct((t, LANES), f32)],
        scratch_shapes=[pltpu.VMEM((tm, D_MODEL), bf16)],
        compiler_params=_cparams(("parallel", "arbitrary")),
        name="inproj",
    )(x2, g, w_main, w_small)


HALO = 16
PREP_ROWS = 256
BETA_LANE = 0
CUM_LANE = 4 * A_HEADS


def _prep_kernel(cur_ref, prev_ref, next_ref, w_ref, sm_ref, par_ref, o_ref, gb_ref, a_ref, *, ts, n_tiles):
    i = pl.program_id(1)
    first = (i > 0).astype(f32)
    last = (i < n_tiles - 1).astype(f32)
    pad = (A_CONV - 1) // 2
    r = lax.broadcasted_iota(i32, (LANES, LANES), 0) // HEAD_DIM
    cc = lax.broadcasted_iota(i32, (LANES, LANES), 1) // HEAD_DIM
    head_ones = (r == cc).astype(f32)
    sub = min(PREP_ROWS, ts)
    for c in range(3 * A_WIDTH // LANES):
        cols = slice(c * LANES, (c + 1) * LANES)
        a_ref[0:HALO, :] = prev_ref[0, :, cols].astype(f32) * first
        a_ref[HALO:HALO + ts, :] = cur_ref[0, :, cols].astype(f32)
        a_ref[HALO + ts:2 * HALO + ts, :] = next_ref[0, :, cols].astype(f32) * last
        w = w_ref[:, cols]

        def rows_body(k, carry, c=c, cols=cols, w=w):
            r0 = pl.multiple_of(k * sub, sub)
            y = jnp.zeros((sub, LANES), f32)
            for j in range(A_CONV):
                y = y + a_ref[pl.ds(r0 + (HALO - pad + j), sub), :] * w[j:j + 1]
            y = y * _sigmoid(y)
            if c < 2 * A_WIDTH // LANES:
                ss = jnp.dot(y * y, head_ones, preferred_element_type=f32, precision=HIGHEST)
                qscale = HEAD_DIM ** -0.5 if c < A_WIDTH // LANES else 1.0
                y = y * (lax.rsqrt(ss + EPS) * qscale)
            o_ref[0, pl.ds(r0, sub), cols] = y.astype(o_ref.dtype)
            return carry

        lax.fori_loop(0, ts // sub, rows_body, 0)

    ch = CHUNK
    ri = lax.broadcasted_iota(i32, (ch, ch), 0)
    ci = lax.broadcasted_iota(i32, (ch, ch), 1)
    lower = (ri >= ci).astype(f32)
    upper = (ri <= ci).astype(f32)
    lane = lax.broadcasted_iota(i32, (ch, LANES), 1)
    g_lane = 2 * A_HEADS
    is_fwd = lane < g_lane + A_HEADS
    for k in range(ts // ch):
        rows = slice(k * ch, (k + 1) * ch)
        sm = sm_ref[0, rows, :]
        g = -par_ref[0:1, :] * _softplus(sm + par_ref[1:2, :])
        cum = jnp.where(is_fwd,
                        jnp.dot(lower, g, preferred_element_type=f32, precision=HIGHEST),
                        jnp.dot(upper, g, preferred_element_type=f32, precision=HIGHEST))
        cum = pltpu.roll(cum, CUM_LANE - g_lane, axis=1)
        gb_ref[0, rows, :] = jnp.where(lane < g_lane, _sigmoid(sm), cum)


def _prep(main3, small3, conv_w, par):
    b, s, _ = main3.shape
    ts = min(1024, s)
    n_tiles = s // ts
    hb = ts // HALO
    wa = 3 * A_WIDTH
    return pl.pallas_call(
        functools.partial(_prep_kernel, ts=ts, n_tiles=n_tiles),
        grid=(b, n_tiles),
        in_specs=[
            pl.BlockSpec((1, ts, wa), lambda bi, i: (bi, i, 0)),
            pl.BlockSpec((1, HALO, wa), lambda bi, i: (bi, jnp.maximum(i * hb - 1, 0), 0)),
            pl.BlockSpec((1, HALO, wa), lambda bi, i: (bi, jnp.minimum((i + 1) * hb, s // HALO - 1), 0)),
            pl.BlockSpec((A_CONV, wa), lambda bi, i: (0, 0)),
            pl.BlockSpec((1, ts, LANES), lambda bi, i: (bi, i, 0)),
            pl.BlockSpec((8, LANES), lambda bi, i: (0, 0)),
        ],
        out_specs=[pl.BlockSpec((1, ts, wa), lambda bi, i: (bi, i, 0)),
                   pl.BlockSpec((1, ts, LANES), lambda bi, i: (bi, i, 0))],
        out_shape=[jax.ShapeDtypeStruct((b, s, wa), bf16), jax.ShapeDtypeStruct((b, s, LANES), f32)],
        scratch_shapes=[pltpu.VMEM((ts + 2 * HALO, LANES), f32)],
        compiler_params=_cparams(("parallel", "parallel")),
        name="prep",
    )(main3, main3, main3, conv_w, small3, par)


GROUP = 8


def _split_heads(x, mask):
    z = jnp.zeros_like(x)
    return jnp.concatenate([jnp.where(mask, x, z), jnp.where(mask, z, x)], axis=0)


def _delta_stages(pair):
    c = CHUNK
    lane = lax.broadcasted_iota(i32, (c, LANES), 1)
    m0 = lane < HEAD_DIM
    m0w = jnp.concatenate([m0, m0], axis=1)
    rl = lax.broadcasted_iota(i32, (c, LANES), 0)
    cl = lane % HEAD_DIM
    eye_p = (rl == cl).astype(f32)
    same_head = (lax.broadcasted_iota(i32, (LANES, LANES), 0) // HEAD_DIM
                 == lax.broadcasted_iota(i32, (LANES, LANES), 1) // HEAD_DIM)
    bd = lambda x: _split_heads(x, m0).astype(bf16)
    bdw = lambda x: _split_heads(x, m0w).astype(bf16)

    def pick(full, base):
        c0 = jnp.sum(jnp.where(lane == base + 2 * pair, full, 0.0), axis=1, keepdims=True)
        c1 = jnp.sum(jnp.where(lane == base + 2 * pair + 1, full, 0.0), axis=1, keepdims=True)
        return jnp.where(m0, c0, c1)

    def gates(st):
        d = 1 if st["upper"] else 0
        st["beta"] = pick(st["gb"], BETA_LANE + d * A_HEADS)
        st["cum"] = pick(st["gb"], CUM_LANE + d * A_HEADS)

    def gram(st):
        q, k = st["q"], st["k"]
        st["gq"] = lax.dot_general(jnp.concatenate([q, k], axis=0), _split_heads(k, m0), (((1,), (1,)), ((), ())),
                                   preferred_element_type=f32)

    def decay(st):
        cum = st["cum"]
        cum_t = cum.T
        cum_row = jnp.concatenate([cum_t[0:1], cum_t[HEAD_DIM:HEAD_DIM + 1]], axis=1)
        incl = (rl <= cl) if st["upper"] else (rl >= cl)
        strict = (rl < cl) if st["upper"] else (rl > cl)
        dec = jnp.where(incl, jnp.exp(jnp.minimum(cum - cum_row, 0.0)), 0.0)
        gq = st.pop("gq")
        st["attn"] = (gq[0:c] * dec).astype(bf16)
        a = jnp.where(strict, -(st["beta"] * gq[c:2 * c] * dec), 0.0)
        st["p"] = eye_p + a
        st["a"] = jnp.dot(a.astype(bf16), bd(a), preferred_element_type=f32)

    def double(st):
        a, p = st["a"], st["p"]
        y = jnp.dot(jnp.concatenate([a, p], axis=0).astype(bf16), bd(a), preferred_element_type=f32)
        st["a"] = y[0:c]
        st["p"] = p + y[c:2 * c]

    def solve(st):
        a, p = st.pop("a"), st.pop("p")
        tinv = (p + jnp.dot(p.astype(bf16), bd(a), preferred_element_type=f32)).astype(bf16)
        cum = st["cum"]
        st["gtot"] = cum[0:1] if st["upper"] else cum[c - 1:c]
        st["eg"] = jnp.exp(cum)
        kf = st["k"].astype(f32)
        rhs = jnp.concatenate([st["v"].astype(f32) * st["beta"], kf * st["beta"] * st["eg"]], axis=1)
        st["uw"] = jnp.dot(tinv, bdw(rhs), preferred_element_type=f32)

    def finish(st):
        uw = st.pop("uw")
        aw = jnp.dot(st.pop("attn"), bdw(uw), preferred_element_type=f32)
        kd = (st["k"].astype(f32) * jnp.exp(st["gtot"] - st["cum"])).astype(bf16)
        t = lax.dot_general(kd, uw.astype(bf16), (((0,), (0,)), ((), ())), preferred_element_type=f32)
        st["oin"] = aw[:, 0:LANES]
        st["qeff"] = (st["q"].astype(f32) * st["eg"] - aw[:, LANES:2 * LANES]).astype(bf16)
        st["bbd"] = jnp.where(same_head, t[:, 0:LANES], 0.0)
        st["abd"] = jnp.where(same_head, -t[:, LANES:2 * LANES], 0.0).astype(bf16)
        st["dec"] = jnp.exp(st["gtot"])

    n_double = int(np.log2(c)) - 2
    return [gates, gram, decay] + [double] * n_double + [solve, finish]


def _delta_kernel(q_ref, k_ref, v_ref, z_ref, gb_ref, an_ref, o_ref,
                  acc_ref, st_ref, qe_ref, oi_ref, ab_ref, bb_ref, dc_ref, *, s):
    c = CHUNK
    n = s // c
    g = min(GROUP, n)
    ng = n // g
    pair = pl.program_id(1)
    acc_ref[...] = jnp.zeros_like(acc_ref)
    st_ref[...] = jnp.zeros_like(st_ref)
    stages = _delta_stages(pair)

    def row_start(gi, t, d):
        cidx = gi * g + t if d == 0 else n - 1 - (gi * g + t)
        return pl.multiple_of(cidx * c, c)

    def group_step(gi_a, slot_a, gi_b, slot_b):
        streams = []
        if gi_a is not None:
            for t in range(g):
                for d in range(2):
                    r0 = row_start(gi_a, t, d)
                    streams.append(dict(
                        q=q_ref[0, pl.ds(r0, c), :], k=k_ref[0, pl.ds(r0, c), :], v=v_ref[0, pl.ds(r0, c), :],
                        gb=gb_ref[0, pl.ds(r0, c), :], upper=(d == 1), idx=(d * 2 + slot_a) * g + t))
        states = [st_ref[0], st_ref[1]] if gi_b is not None else None

        def recurrence(t):
            for d in range(2):
                r0 = row_start(gi_b, t, d)
                idx = (d * 2 + slot_b) * g + t
                y = jnp.dot(jnp.concatenate([ab_ref[idx], qe_ref[idx]], axis=0), states[d].astype(bf16),
                            preferred_element_type=f32)
                acc_ref[pl.ds(r0, c), :] += y[LANES:LANES + c] + oi_ref[idx]
                states[d] = states[d] * dc_ref[idx][0:1] + y[0:LANES] + bb_ref[idx]

        done_b = 0
        for si, stage in enumerate(stages):
            for st in streams:
                stage(st)
            if gi_b is not None and si >= 1 and done_b < g:
                recurrence(done_b)
                done_b += 1
        if gi_b is not None:
            for t in range(done_b, g):
                recurrence(t)
            st_ref[0] = states[0]
            st_ref[1] = states[1]
        for st in streams:
            idx = st["idx"]
            qe_ref[idx] = st["qeff"]
            oi_ref[idx] = st["oin"]
            ab_ref[idx] = st["abd"]
            bb_ref[idx] = st["bbd"]
            dc_ref[idx] = jnp.broadcast_to(st["dec"], (8, LANES))

    group_step(0, 0, None, None)

    def body(i, carry):
        group_step(i, i % 2, i - 1, (i - 1) % 2)
        return carry

    lax.fori_loop(1, ng, body, 0)
    group_step(None, None, ng - 1, (ng - 1) % 2)

    rows = min(256, s)
    r = lax.broadcasted_iota(i32, (LANES, LANES), 0) // HEAD_DIM
    cc = lax.broadcasted_iota(i32, (LANES, LANES), 1) // HEAD_DIM
    head_mean = (r == cc).astype(f32) * (1.0 / HEAD_DIM)

    def epi(i, carry):
        r0 = pl.multiple_of(i * rows, rows)
        o = acc_ref[pl.ds(r0, rows), :]
        ms = jnp.dot(o * o, head_mean, preferred_element_type=f32, precision=HIGHEST)
        z = z_ref[0, pl.ds(r0, rows), :].astype(f32)
        y = o * lax.rsqrt(ms + EPS) * an_ref[...] * (z * _sigmoid(z))
        o_ref[0, pl.ds(r0, rows), :] = y.astype(o_ref.dtype)
        return carry

    lax.fori_loop(0, s // rows, epi, 0)


def _delta(qkvn, main3, gb3, an):
    b, s, _ = qkvn.shape
    npair = A_HEADS // 2
    kb = A_WIDTH // LANES
    nbuf = 4 * min(GROUP, s // CHUNK)
    seq = lambda off: pl.BlockSpec((1, s, LANES), lambda bi, p: (bi, 0, off + p))
    return pl.pallas_call(
        functools.partial(_delta_kernel, s=s),
        grid=(b, npair),
        in_specs=[
            seq(0), seq(kb), seq(2 * kb), seq(Z_BLK),
            pl.BlockSpec((1, s, LANES), lambda bi, p: (bi, 0, 0)),
            pl.BlockSpec((1, LANES), lambda bi, p: (0, 0)),
        ],
        out_specs=pl.BlockSpec((1, s, LANES), lambda bi, p: (bi, 0, p)),
        out_shape=jax.ShapeDtypeStruct((b, s, A_WIDTH), bf16),
        scratch_shapes=[pltpu.VMEM((s, LANES), f32), pltpu.VMEM((2, LANES, LANES), f32),
                        pltpu.VMEM((nbuf, CHUNK, LANES), bf16), pltpu.VMEM((nbuf, CHUNK, LANES), f32),
                        pltpu.VMEM((nbuf, LANES, LANES), bf16), pltpu.VMEM((nbuf, LANES, LANES), f32),
                        pltpu.VMEM((nbuf, 8, LANES), f32)],
        compiler_params=_cparams(("parallel", "arbitrary")),
        name="delta",
    )(qkvn, qkvn, qkvn, main3, gb3, an)


def _attn_kernel(q_ref, k_ref, v_ref, o_ref, l_ref, *, lp, bq, nk, dil, slopes):
    w = B_GROUP_WIDTH
    head_of_lane = lax.broadcasted_iota(i32, (1, w), 1) // HEAD_DIM
    scale = HEAD_DIM ** -0.5

    def body(i, carry):
        q0 = pl.multiple_of(i * bq, bq)
        ks = pl.multiple_of(jnp.clip(q0 - B_SIDE, 0, lp - nk), B_SIDE)
        q = q_ref[0, pl.ds(q0, bq), :]
        k = k_ref[0, pl.ds(ks, nk), :]
        v = v_ref[0, pl.ds(ks, nk), :]
        zq = jnp.zeros_like(q)
        qs = jnp.concatenate([jnp.where(head_of_lane == h, q, zq) for h in range(B_HEADS_PER_GROUP)], axis=0)
        sc = lax.dot_general(qs, k, (((1,), (1,)), ((), ())), preferred_element_type=f32) * scale
        qpos = q0 + lax.broadcasted_iota(i32, (bq, nk), 0)
        kpos = ks + lax.broadcasted_iota(i32, (bq, nk), 1)
        adelta = jnp.abs(kpos - qpos)
        valid = adelta <= B_SIDE
        dist = adelta.astype(f32) * float(dil)
        o = jnp.zeros((bq, w), f32)
        lse = jnp.zeros((bq, w), f32)
        for h in range(B_HEADS_PER_GROUP):
            sh = jnp.where(valid, sc[h * bq:(h + 1) * bq] - slopes[h] * dist, NEG)
            m = jnp.max(sh, axis=1, keepdims=True)
            p = jnp.exp(sh - m)
            l = jnp.sum(p, axis=1, keepdims=True)
            oh = jnp.dot(p.astype(bf16), v, preferred_element_type=f32) / l
            hm = head_of_lane == h
            o = jnp.where(hm, oh, o)
            lse = jnp.where(hm, m + jnp.log(l), lse)
        o_ref[0, pl.ds(q0, bq), :] = o.astype(o_ref.dtype)
        l_ref[0, pl.ds(q0, bq), :] = lse
        return carry

    lax.fori_loop(0, lp // bq, body, 0)


def _attention(main3, group):
    b, s, _ = main3.shape
    _, dil = B_GROUPS[group]
    lp = s // dil
    bq = min(128, lp)
    nk = min(bq + 2 * B_SIDE, lp)
    slopes = tuple(float(2.0 ** (-8.0 * (group * B_HEADS_PER_GROUP + h + 1) / B_HEADS))
                   for h in range(B_HEADS_PER_GROUP))
    w = B_GROUP_WIDTH
    ngrp = len(B_GROUPS)
    blk = lambda off: QKV_B_BLK256 + off * ngrp + group
    if dil == 1:
        src = main3
        spec = lambda off: pl.BlockSpec((1, lp, w), lambda bi, r: (bi, 0, blk(off)))
    else:
        qkv = jnp.concatenate([main3[:, :, blk(off) * w:(blk(off) + 1) * w] for off in range(3)], axis=-1)
        src = qkv.reshape(b, lp, dil * 3 * w)
        spec = lambda off: pl.BlockSpec((1, lp, w), lambda bi, r: (bi, 0, r * 3 + off))
    o, lse = pl.pallas_call(
        functools.partial(_attn_kernel, lp=lp, bq=bq, nk=nk, dil=dil, slopes=slopes),
        grid=(b, dil),
        in_specs=[spec(0), spec(1), spec(2)],
        out_specs=[pl.BlockSpec((1, lp, w), lambda bi, r: (bi, 0, r)),
                   pl.BlockSpec((1, lp, w), lambda bi, r: (bi, 0, r))],
        out_shape=[jax.ShapeDtypeStruct((b, lp, dil * w), bf16), jax.ShapeDtypeStruct((b, lp, dil * w), f32)],
        compiler_params=_cparams(("parallel", "parallel")),
        name=f"attn_d{dil}",
    )(src, src, src)
    return o.reshape(b * s, w), lse.reshape(b * s, w)


def _merge_kernel(x_ref, oa_ref, o1_ref, o2_ref, o3_ref, l1_ref, l2_ref, l3_ref, ga_ref, gb_ref,
                  wpa_ref, wpb_ref, wout_ref, out_ref):
    l1, l2, l3 = l1_ref[...], l2_ref[...], l3_ref[...]
    m = jnp.maximum(jnp.maximum(l1, l2), l3)
    e1, e2, e3 = jnp.exp(l1 - m), jnp.exp(l2 - m), jnp.exp(l3 - m)
    ob = (e1 * o1_ref[...].astype(f32) + e2 * o2_ref[...].astype(f32) + e3 * o3_ref[...].astype(f32)) / (e1 + e2 + e3)
    ya = jnp.dot(oa_ref[...], wpa_ref[...], preferred_element_type=f32)
    yb = jnp.dot(ob.astype(bf16), wpb_ref[...], preferred_element_type=f32)
    mix = _sigmoid(ga_ref[...].astype(f32)) * ya + _sigmoid(gb_ref[...].astype(f32)) * yb
    out_ref[...] = x_ref[...] + jnp.dot(mix.astype(bf16), wout_ref[...], preferred_element_type=f32)


def _merge(x2, oa2, obs, lses, main2, wpa, wpb, wout):
    t = x2.shape[0]
    tm = min(512, t)
    w = B_GROUP_WIDTH
    row = lambda width: pl.BlockSpec((tm, width), lambda i: (i, 0))
    full = lambda a, bb: pl.BlockSpec((a, bb), lambda i: (0, 0))
    return pl.pallas_call(
        _merge_kernel,
        grid=(t // tm,),
        in_specs=[row(D_MODEL), row(A_WIDTH), row(w), row(w), row(w), row(w), row(w), row(w),
                  pl.BlockSpec((tm, D_MODEL), lambda i: (i, GATE_BLK)),
                  pl.BlockSpec((tm, D_MODEL), lambda i: (i, GATE_BLK + 1)),
                  full(A_WIDTH, D_MODEL), full(w, D_MODEL), full(D_MODEL, D_MODEL)],
        out_specs=row(D_MODEL),
        out_shape=jax.ShapeDtypeStruct((t, D_MODEL), f32),
        compiler_params=_cparams(("parallel",)),
        name="merge",
    )(x2, oa2, *obs, *lses, main2, main2, wpa, wpb, wout)


def _router_kernel(x_ref, g_ref, wr_ref, hn_ref, aff_ref, afft_ref):
    x = x_ref[...]
    ms = jnp.mean(x * x, axis=-1, keepdims=True)
    hn = x * lax.rsqrt(ms + EPS) * g_ref[...]
    hn_ref[...] = hn.astype(bf16)
    hi = hn.astype(bf16)
    lo = (hn - hi.astype(f32)).astype(bf16)
    tm = hn.shape[0]
    prod = jnp.dot(jnp.concatenate([hi, lo], axis=0), wr_ref[...], preferred_element_type=f32)
    logits = (prod[0:tm, 0:LANES] + prod[0:tm, LANES:2 * LANES]) + prod[tm:2 * tm, 0:LANES]
    lane = lax.broadcasted_iota(i32, logits.shape, 1)
    logits = jnp.where(lane < N_EXPERTS, logits, NEG)
    m = jnp.max(logits, axis=1, keepdims=True)
    e = jnp.exp(logits - m)
    aff = e / jnp.sum(e, axis=1, keepdims=True)
    aff_ref[...] = aff
    afft_ref[...] = aff.T[0:N_EXPERTS]


def _router(x2, g, wr):
    t = x2.shape[0]
    tm = min(512, t)
    return pl.pallas_call(
        _router_kernel,
        grid=(t // tm,),
        in_specs=[pl.BlockSpec((tm, D_MODEL), lambda i: (i, 0)),
                  pl.BlockSpec((1, D_MODEL), lambda i: (0, 0)),
                  pl.BlockSpec((D_MODEL, 2 * LANES), lambda i: (0, 0))],
        out_specs=[pl.BlockSpec((tm, D_MODEL), lambda i: (i, 0)),
                   pl.BlockSpec((tm, LANES), lambda i: (i, 0)),
                   pl.BlockSpec((N_EXPERTS, tm), lambda i: (0, i))],
        out_shape=[jax.ShapeDtypeStruct((t, D_MODEL), bf16), jax.ShapeDtypeStruct((t, LANES), f32),
                   jax.ShapeDtypeStruct((N_EXPERTS, t), f32)],
        compiler_params=_cparams(("parallel",)),
        name="router",
    )(x2, g, wr)


def _threshold_kernel(afft_ref, thr_ref, need_ref, *, cap):
    bits = lax.bitcast_convert_type(afft_ref[...], i32)

    def step(i, lo):
        cand = lo | lax.shift_left(jnp.int32(1), 30 - i)
        cnt = jnp.sum((bits >= cand).astype(i32), axis=1, keepdims=True)
        return jnp.where(cnt >= cap, cand, lo)

    thr = lax.fori_loop(0, 31, step, jnp.zeros((N_EXPERTS, 1), i32))
    n_gt = jnp.sum((bits > thr).astype(i32), axis=1, keepdims=True)
    thr_ref[...] = jnp.broadcast_to(thr, (N_EXPERTS, LANES))
    need_ref[...] = jnp.broadcast_to(cap - n_gt, (N_EXPERTS, LANES))


def _threshold(afft, cap):
    return pl.pallas_call(
        functools.partial(_threshold_kernel, cap=cap),
        out_shape=[jax.ShapeDtypeStruct((N_EXPERTS, LANES), i32), jax.ShapeDtypeStruct((N_EXPERTS, LANES), i32)],
        compiler_params=pltpu.CompilerParams(vmem_limit_bytes=VMEM_LIMIT),
        name="threshold",
    )(afft)


def _select_kernel(aff_ref, thr_ref, need_ref, rk_ref, tb_ref, carry_ref):
    tt = SEL_TILE

    @pl.when(pl.program_id(0) == 0)
    def _():
        carry_ref[...] = jnp.zeros_like(carry_ref)

    bits = lax.bitcast_convert_type(aff_ref[...], i32)
    lane_ok = lax.broadcasted_iota(i32, (tt, LANES), 1) < N_EXPERTS
    thr = thr_ref[...]
    gt = (bits > thr) & lane_ok
    eq = (bits == thr) & lane_ok
    below = (lax.broadcasted_iota(i32, (tt, tt), 0) > lax.broadcasted_iota(i32, (tt, tt), 1)).astype(bf16)
    tie_carry = carry_ref[0:1, :]
    base = carry_ref[1:2, :]
    eqf = eq.astype(f32)
    tie_before = jnp.dot(below, eq.astype(bf16), preferred_element_type=f32) + tie_carry
    sel = gt | (eq & (tie_before < need_ref[...].astype(f32)))
    self_ = sel.astype(f32)
    rank = jnp.dot(below, sel.astype(bf16), preferred_element_type=f32)
    n = jnp.sum(self_, axis=0, keepdims=True)
    rk_ref[...] = jnp.where(sel, rank, -1.0)
    tb_ref[0, 0:1, :] = base.astype(i32)
    tb_ref[0, 1:2, :] = n.astype(i32)
    tb_ref[0, 2:8, :] = jnp.zeros((6, LANES), i32)
    carry_ref[0:1, :] = tie_carry + jnp.sum(eqf, axis=0, keepdims=True)
    carry_ref[1:2, :] = base + n


def _select(aff, thr_row, need_row):
    t = aff.shape[0]
    nt = t // SEL_TILE
    return pl.pallas_call(
        _select_kernel,
        grid=(nt,),
        in_specs=[pl.BlockSpec((SEL_TILE, LANES), lambda j: (j, 0)),
                  pl.BlockSpec((1, LANES), lambda j: (0, 0)),
                  pl.BlockSpec((1, LANES), lambda j: (0, 0))],
        out_specs=[pl.BlockSpec((SEL_TILE, LANES), lambda j: (j, 0)),
                   pl.BlockSpec((1, 8, LANES), lambda j: (j, 0, 0))],
        out_shape=[jax.ShapeDtypeStruct((t, LANES), f32), jax.ShapeDtypeStruct((nt, 8, LANES), i32)],
        scratch_shapes=[pltpu.VMEM((8, LANES), f32)],
        compiler_params=_cparams(("arbitrary",)),
        name="select",
    )(aff, thr_row, need_row)


WIN = SMALL_BUCKET + ROW_ALIGN
BIG_WIN = SEL_TILE + ROW_ALIGN


def _buckets(n):
    return ((WIN, (n > 0) & (n <= SMALL_BUCKET)), (BIG_WIN, n > SMALL_BUCKET))


def _aligned(x):
    return pl.multiple_of(x - x % ROW_ALIGN, ROW_ALIGN)


def _dispatch_kernel(base_ref, cnt_ref, fast_ref, rk_ref, hn_ref, xe_ref,
                     stage_ref, big_ref, carry_ref, sem_ref, bsem_ref, *, cap):
    j = pl.program_id(0)
    nt = pl.num_programs(0)
    tt = SEL_TILE
    ne = N_EXPERTS
    slot = j % 2

    @pl.when(j == 0)
    def _():
        carry_ref[...] = jnp.zeros_like(carry_ref)
        big_ref[0] = jnp.zeros((BIG_WIN, D_MODEL), bf16)
        tail = lambda e: pltpu.make_async_copy(big_ref.at[0], xe_ref.at[e, pl.ds(cap, BIG_WIN)], bsem_ref.at[e])
        for e in range(ne):
            tail(e).start()
        for e in range(ne):
            tail(e).wait()

    rkt = rk_ref[...].T
    hn = hn_ref[...]

    def targets(e):
        rem = base_ref[j * ne + e] % ROW_ALIGN
        row = rkt[e:e + 1, :]
        return jnp.where(row >= 0.0, row + rem.astype(f32), -1.0)

    def merge_carry(ref_rows, e):
        n = cnt_ref[j * ne + e]
        rem = base_ref[j * ne + e] % ROW_ALIGN
        head = ref_rows(0, ROW_ALIGN)
        head[...] += carry_ref[e]
        keep = pl.multiple_of(((rem + n) // ROW_ALIGN) * ROW_ALIGN, ROW_ALIGN)
        carry_ref[e] = ref_rows(keep, ROW_ALIGN)[...]

    def fast_copy(jj, sl, e):
        return pltpu.make_async_copy(stage_ref.at[sl, pl.ds(e * WIN, WIN)],
                                     xe_ref.at[e, pl.ds(_aligned(base_ref[jj * ne + e]), WIN)], sem_ref.at[e])

    def wait_previous():
        jp = jnp.maximum(j - 1, 0)

        @pl.when((j > 0) & (fast_ref[jp] == 1))
        def _():
            for e in range(ne):
                @pl.when(cnt_ref[jp * ne + e] > 0)
                def _(e=e):
                    fast_copy(jp, 1 - slot, e).wait()

    @pl.when(fast_ref[j] == 1)
    def _():
        win_slot = lax.broadcasted_iota(i32, (WIN, tt), 0).astype(f32)
        onehot = jnp.concatenate([(win_slot == targets(e)).astype(bf16) for e in range(ne)], axis=0)
        stage_ref[slot] = jnp.dot(onehot, hn, preferred_element_type=f32).astype(bf16)
        for e in range(ne):
            @pl.when(cnt_ref[j * ne + e] > 0)
            def _(e=e):
                merge_carry(lambda st, sz: stage_ref.at[slot, pl.ds(e * WIN + st, sz)], e)
        wait_previous()
        for e in range(ne):
            @pl.when(cnt_ref[j * ne + e] > 0)
            def _(e=e):
                fast_copy(j, slot, e).start()

        @pl.when(j == nt - 1)
        def _():
            for e in range(ne):
                @pl.when(cnt_ref[j * ne + e] > 0)
                def _(e=e):
                    fast_copy(j, slot, e).wait()

    @pl.when(fast_ref[j] == 0)
    def _():
        wait_previous()
        big_copy = lambda e, rows: pltpu.make_async_copy(
            big_ref.at[e, pl.ds(0, rows)], xe_ref.at[e, pl.ds(_aligned(base_ref[j * ne + e]), rows)], bsem_ref.at[e])
        for e in range(ne):
            for rows, cond in _buckets(cnt_ref[j * ne + e]):
                @pl.when(cond)
                def _(rows=rows, e=e):
                    win_slot = lax.broadcasted_iota(i32, (rows, tt), 0).astype(f32)
                    onehot = (win_slot == targets(e)).astype(bf16)
                    big_ref[e, 0:rows, :] = jnp.dot(onehot, hn, preferred_element_type=f32).astype(bf16)
                    merge_carry(lambda st, sz: big_ref.at[e, pl.ds(st, sz)], e)
                    big_copy(e, rows).start()
        for e in range(ne):
            for rows, cond in _buckets(cnt_ref[j * ne + e]):
                @pl.when(cond)
                def _(rows=rows, e=e):
                    big_copy(e, rows).wait()


def _dispatch(base, cnt, fast, rk, hn, cap):
    t = hn.shape[0]
    nt = t // SEL_TILE
    return pl.pallas_call(
        functools.partial(_dispatch_kernel, cap=cap),
        grid_spec=pltpu.PrefetchScalarGridSpec(
            num_scalar_prefetch=3,
            grid=(nt,),
            in_specs=[pl.BlockSpec((SEL_TILE, LANES), lambda j, b, c, f: (j, 0)),
                      pl.BlockSpec((SEL_TILE, D_MODEL), lambda j, b, c, f: (j, 0))],
            out_specs=pl.BlockSpec(memory_space=pl.ANY),
            scratch_shapes=[pltpu.VMEM((2, N_EXPERTS * WIN, D_MODEL), bf16),
                            pltpu.VMEM((N_EXPERTS, BIG_WIN, D_MODEL), bf16),
                            pltpu.VMEM((N_EXPERTS, ROW_ALIGN, D_MODEL), bf16),
                            pltpu.SemaphoreType.DMA((N_EXPERTS,)),
                            pltpu.SemaphoreType.DMA((N_EXPERTS,))],
        ),
        out_shape=jax.ShapeDtypeStruct((N_EXPERTS, cap + BIG_WIN, D_MODEL), bf16),
        compiler_params=_cparams(("arbitrary",)),
        name="dispatch",
    )(base, cnt, fast, rk, hn)


def _ffn_kernel(x_ref, wg_ref, wu_ref, wd_ref, y_ref):
    x = x_ref[0]
    g = jnp.dot(x, wg_ref[0], preferred_element_type=f32)
    u = jnp.dot(x, wu_ref[0], preferred_element_type=f32)
    h = (g * _sigmoid(g) * u).astype(bf16)
    y_ref[0] = jnp.dot(h, wd_ref[0], preferred_element_type=f32).astype(y_ref.dtype)


def _ffn(xe, wg, wu, wd, cap):
    tr = min(512, cap)
    wspec = lambda a, bb: pl.BlockSpec((1, a, bb), lambda e, i: (e, 0, 0))
    return pl.pallas_call(
        _ffn_kernel,
        grid=(N_EXPERTS, cap // tr),
        in_specs=[pl.BlockSpec((1, tr, D_MODEL), lambda e, i: (e, i, 0)),
                  wspec(D_MODEL, D_EXPERT), wspec(D_MODEL, D_EXPERT), wspec(D_EXPERT, D_MODEL)],
        out_specs=pl.BlockSpec((1, tr, D_MODEL), lambda e, i: (e, i, 0)),
        out_shape=jax.ShapeDtypeStruct((N_EXPERTS, cap, D_MODEL), bf16),
        compiler_params=_cparams(("parallel", "parallel")),
        name="expert_ffn",
    )(xe, wg, wu, wd)


def _combine_kernel(base_ref, cnt_ref, fast_ref, rk_ref, aff_ref, x_ref, gf_ref, ye_ref, out_ref,
                    buf_ref, big_ref, sem_ref, bsem_ref, *, cap, final_norm):
    j = pl.program_id(0)
    nt = pl.num_programs(0)
    tt = SEL_TILE
    ne = N_EXPERTS
    slot = j % 2
    win = min(WIN, cap)
    window_start = lambda base, rows: pl.multiple_of(jnp.minimum(base - base % ROW_ALIGN, cap - rows), ROW_ALIGN)

    def win_copy(jj, sl, e):
        return pltpu.make_async_copy(ye_ref.at[e, pl.ds(window_start(base_ref[jj * ne + e], win), win)],
                                     buf_ref.at[sl, pl.ds(e * win, win)], sem_ref.at[sl])

    @pl.when((j == 0) & (fast_ref[0] == 1))
    def _():
        for e in range(ne):
            win_copy(0, 0, e).start()

    jn = jnp.minimum(j + 1, nt - 1)

    @pl.when((j + 1 < nt) & (fast_ref[jn] == 1))
    def _():
        for e in range(ne):
            win_copy(jn, 1 - slot, e).start()

    @pl.when(fast_ref[j] == 1)
    def _():
        rkt = rk_ref[...].T
        afft = aff_ref[...].T
        win_slot = lax.broadcasted_iota(i32, (win, tt), 0).astype(f32)
        his, los = [], []
        for e in range(ne):
            base = base_ref[j * ne + e]
            shift = (base - window_start(base, win)).astype(f32)
            row = rkt[e:e + 1, :]
            gate = jnp.where((win_slot == row + shift) & (row >= 0.0), afft[e:e + 1, :], 0.0)
            hi = gate.astype(bf16)
            his.append(hi)
            los.append((gate - hi.astype(f32)).astype(bf16))
        lhs = jnp.concatenate([jnp.concatenate(his, axis=0), jnp.concatenate(los, axis=0)], axis=1)
        for e in range(ne):
            win_copy(j, slot, e).wait()
        y = lax.dot_general(lhs, buf_ref[slot], (((0,), (0,)), ((), ())), preferred_element_type=f32)
        out_ref[...] = x_ref[...] + y[0:tt] + y[tt:2 * tt]

    @pl.when(fast_ref[j] == 0)
    def _():
        buckets = lambda n: tuple((min(rows, cap), cond) for rows, cond in _buckets(n))
        big_copy = lambda e, rows: pltpu.make_async_copy(
            ye_ref.at[e, pl.ds(window_start(base_ref[j * ne + e], rows), rows)], big_ref.at[e, pl.ds(0, rows)],
            bsem_ref.at[e])
        for e in range(ne):
            for rows, cond in buckets(cnt_ref[j * ne + e]):
                @pl.when(cond)
                def _(rows=rows, e=e):
                    big_copy(e, rows).start()
        out_ref[...] = x_ref[...]
        for e in range(ne):
            for rows, cond in buckets(cnt_ref[j * ne + e]):
                @pl.when(cond)
                def _(rows=rows, e=e):
                    base = base_ref[j * ne + e]
                    big_copy(e, rows).wait()
                    col = rk_ref[:, e:e + 1]
                    tgt = col + (base - window_start(base, rows)).astype(f32)
                    win_slot = lax.broadcasted_iota(i32, (tt, rows), 1).astype(f32)
                    onehot = ((tgt == win_slot) & (col >= 0.0)).astype(bf16)
                    contrib = jnp.dot(onehot, big_ref[e, 0:rows, :], preferred_element_type=f32)
                    out_ref[...] += contrib * aff_ref[:, e:e + 1]

    if final_norm:
        y = out_ref[...]
        ms = jnp.mean(y * y, axis=-1, keepdims=True)
        out_ref[...] = y * lax.rsqrt(ms + EPS) * gf_ref[...]


def _combine(base, cnt, fast, rk, aff, x2, gfin, ye, cap, final_norm):
    t = x2.shape[0]
    nt = t // SEL_TILE
    tile = lambda width: pl.BlockSpec((SEL_TILE, width), lambda j, b, c, f: (j, 0))
    return pl.pallas_call(
        functools.partial(_combine_kernel, cap=cap, final_norm=final_norm),
        grid_spec=pltpu.PrefetchScalarGridSpec(
            num_scalar_prefetch=3,
            grid=(nt,),
            in_specs=[tile(LANES), tile(LANES), tile(D_MODEL),
                      pl.BlockSpec((1, D_MODEL), lambda j, b, c, f: (0, 0)),
                      pl.BlockSpec(memory_space=pl.ANY)],
            out_specs=tile(D_MODEL),
            scratch_shapes=[pltpu.VMEM((2, N_EXPERTS * min(WIN, cap), D_MODEL), bf16),
                            pltpu.VMEM((N_EXPERTS, min(BIG_WIN, cap), D_MODEL), bf16),
                            pltpu.SemaphoreType.DMA((2,)),
                            pltpu.SemaphoreType.DMA((N_EXPERTS,))],
        ),
        out_shape=jax.ShapeDtypeStruct((t, D_MODEL), f32),
        compiler_params=_cparams(("arbitrary",)),
        name="combine",
    )(base, cnt, fast, rk, aff, x2, gfin, ye)


def _split_bf16(w):
    hi = w.astype(bf16)
    return jnp.concatenate([hi, (w - hi.astype(f32)).astype(bf16)], axis=1)


def _prep_layer(l, norm_mix, w_in, conv_w, a_log, dt_bias, a_norm, w_proj_a, w_proj_b, w_out,
                norm_ffn, w_router, w_gate, w_up, w_down):
    w = w_in[l]
    small0 = 4 * A_WIDTH
    small1 = small0 + 4 * A_HEADS
    qkvb1 = small1 + 3 * B_HEADS * HEAD_DIM
    w_main = jnp.concatenate([w[:, :small0], w[:, qkvb1:], w[:, small1:qkvb1]], axis=1).astype(bf16)
    w_small = jnp.pad(w[:, small0:small1], ((0, 0), (0, LANES - 4 * A_HEADS))).astype(bf16)
    par = jnp.zeros((8, LANES), f32)
    par = par.at[0, 2 * A_HEADS:4 * A_HEADS].set(jnp.exp(a_log[l].astype(f32)).reshape(-1))
    par = par.at[1, 2 * A_HEADS:4 * A_HEADS].set(dt_bias[l].astype(f32).reshape(-1))
    return dict(
        g_mix=norm_mix[l].reshape(1, D_MODEL), w_main=w_main, w_small=w_small, conv_w=conv_w[l], par=par,
        an=jnp.tile(a_norm[l], 2).reshape(1, LANES),
        wpa=w_proj_a[l].astype(bf16), wpb=w_proj_b[l].astype(bf16), wout=w_out[l].astype(bf16),
        g_ffn=norm_ffn[l].reshape(1, D_MODEL),
        wr=_split_bf16(jnp.pad(w_router[l], ((0, 0), (0, LANES - N_EXPERTS)))),
        wg=w_gate[l].astype(bf16), wu=w_up[l].astype(bf16), wd=w_down[l].astype(bf16),
    )


def _layer(x2, b, s, lw, gfin, final_norm):
    t = b * s
    main2, small2 = _inproj(x2, lw["g_mix"], lw["w_main"], lw["w_small"])
    main3 = main2.reshape(b, s, MAIN_WIDTH)
    qkvn, gb3 = _prep(main3, small2.reshape(b, s, LANES), lw["conv_w"], lw["par"])
    oa = _delta(qkvn, main3, gb3, lw["an"])
    obs, lses = [], []
    for g in range(len(B_GROUPS)):
        o_g, l_g = _attention(main3, g)
        obs.append(o_g)
        lses.append(l_g)
    x2 = _merge(x2, oa.reshape(t, A_WIDTH), obs, lses, main2, lw["wpa"], lw["wpb"], lw["wout"])

    cap = (CAPACITY_FACTOR * t) // N_EXPERTS
    hn, aff, afft = _router(x2, lw["g_ffn"], lw["wr"])
    thr, need = _threshold(afft, cap)
    pad_row = lambda a: jnp.pad(a[:, 0], (0, LANES - N_EXPERTS)).reshape(1, LANES)
    rk, tb = _select(aff, pad_row(thr), pad_row(need))
    base = tb[:, 0, :N_EXPERTS].reshape(-1)
    cnt = tb[:, 1, :N_EXPERTS].reshape(-1)
    fast = (jnp.max(tb[:, 1, :N_EXPERTS], axis=1) <= SMALL_BUCKET).astype(i32)
    xe = _dispatch(base, cnt, fast, rk, hn, cap)
    ye = _ffn(xe, lw["wg"], lw["wu"], lw["wd"], cap)
    return _combine(base, cnt, fast, rk, aff, x2, gfin, ye, cap, final_norm)


def kernel(x_prompt, x_sample, norm_mix, w_in, conv_w, a_log, dt_bias, a_norm, w_proj_a, w_proj_b, w_out,
           norm_ffn, w_router, w_gate, w_up, w_down, norm_final):
    depth = w_in.shape[0]
    layers = [_prep_layer(l, norm_mix, w_in, conv_w, a_log, dt_bias, a_norm, w_proj_a, w_proj_b, w_out,
                          norm_ffn, w_router, w_gate, w_up, w_down) for l in range(depth)]
    gfin = norm_final.reshape(1, D_MODEL)
    outs = []
    for x in (x_prompt, x_sample):
        b, s, d = x.shape
        x2 = x.reshape(b * s, d)
        for l in range(depth):
            x2 = _layer(x2, b, s, layers[l], gfin, l == depth - 1)
        outs.append(x2.reshape(b, s, d))
    return tuple(outs)
```

```python
import functools

import numpy as np
import jax
import jax.numpy as jnp
from jax import lax
from jax.experimental import pallas as pl
from jax.experimental.pallas import tpu as pltpu

f32 = jnp.float32
bf16 = jnp.bfloat16
i32 = jnp.int32
HIGHEST = lax.Precision.HIGHEST

D_MODEL = 1024
A_HEADS = 8
HEAD_DIM = 64
A_WIDTH = A_HEADS * HEAD_DIM
A_CONV = 5
CHUNK = 64
B_GROUPS = ((128, 1), (512, 4), (2048, 16))
B_HEADS_PER_GROUP = 4
B_HEADS = B_HEADS_PER_GROUP * len(B_GROUPS)
B_GROUP_WIDTH = B_HEADS_PER_GROUP * HEAD_DIM
B_SIDE = 64
N_EXPERTS = 16
D_EXPERT = 1024
CAPACITY_FACTOR = 2
EPS = 1e-6
NEG = -1e30

LANES = 128
MAIN_WIDTH = 2 * D_MODEL + 4 * A_WIDTH + 3 * B_HEADS * HEAD_DIM
Z_BLK = 3 * A_WIDTH // LANES
GATE_BLK = 4 * A_WIDTH // D_MODEL
QKV_B_BLK256 = (2 * D_MODEL + 4 * A_WIDTH) // B_GROUP_WIDTH
MAIN_BLK256 = MAIN_WIDTH // B_GROUP_WIDTH
N_TILE = 1280
SEL_TILE = 256
SMALL_BUCKET = 64
ROW_ALIGN = 16
VMEM_LIMIT = 56 * 1024 * 1024


def _cparams(sem):
    return pltpu.CompilerParams(dimension_semantics=sem, vmem_limit_bytes=VMEM_LIMIT)


def _sigmoid(x):
    return 1.0 / (1.0 + jnp.exp(-x))


def _softplus(x):
    return jnp.maximum(x, 0.0) + jnp.log(1.0 + jnp.exp(-jnp.abs(x)))


def _inproj_kernel(x_ref, g_ref, w_ref, ws_ref, o_ref, os_ref, n_ref):
    @pl.when(pl.program_id(1) == 0)
    def _():
        x = x_ref[...]
        ms = jnp.mean(x * x, axis=-1, keepdims=True)
        n = (x * lax.rsqrt(ms + EPS) * g_ref[...]).astype(bf16)
        n_ref[...] = n
        os_ref[...] = jnp.dot(n, ws_ref[...], preferred_element_type=f32)

    o_ref[...] = jnp.dot(n_ref[...], w_ref[...], preferred_element_type=f32).astype(o_ref.dtype)


def _inproj(x2, g, w_main, w_small):
    t = x2.shape[0]
    tm = min(1024, t)
    return pl.pallas_call(
        _inproj_kernel,
        grid=(t // tm, MAIN_WIDTH // N_TILE),
        in_specs=[
            pl.BlockSpec((tm, D_MODEL), lambda i, j: (i, 0)),
            pl.BlockSpec((1, D_MODEL), lambda i, j: (0, 0)),
            pl.BlockSpec((D_MODEL, N_TILE), lambda i, j: (0, j)),
            pl.BlockSpec((D_MODEL, LANES), lambda i, j: (0, 0)),
        ],
        out_specs=[
            pl.BlockSpec((tm, N_TILE), lambda i, j: (i, j)),
            pl.BlockSpec((tm, LANES), lambda i, j: (i, 0)),
        ],
        out_shape=[jax.ShapeDtypeStruct((t, MAIN_WIDTH), bf16), jax.ShapeDtypeStruct((t, LANES), f32)],
        scratch_shapes=[pltpu.VMEM((tm, D_MODEL), bf16)],
        compiler_params=_cparams(("parallel", "arbitrary")),
        name="inproj",
    )(x2, g, w_main, w_small)


HALO = 16
PREP_ROWS = 256
BETA_LANE = 0
CUM_LANE = 4 * A_HEADS


def _prep_kernel(cur_ref, prev_ref, next_ref, w_ref, sm_ref, par_ref, o_ref, gb_ref, a_ref, *, ts, n_tiles):
    i = pl.program_id(1)
    first = (i > 0).astype(f32)
    last = (i < n_tiles - 1).astype(f32)
    pad = (A_CONV - 1) // 2
    r = lax.broadcasted_iota(i32, (LANES, LANES), 0) // HEAD_DIM
    cc = lax.broadcasted_iota(i32, (LANES, LANES), 1) // HEAD_DIM
    head_ones = (r == cc).astype(f32)
    sub = min(PREP_ROWS, ts)
    for c in range(3 * A_WIDTH // LANES):
        cols = slice(c * LANES, (c + 1) * LANES)
        a_ref[0:HALO, :] = prev_ref[0, :, cols].astype(f32) * first
        a_ref[HALO:HALO + ts, :] = cur_ref[0, :, cols].astype(f32)
        a_ref[HALO + ts:2 * HALO + ts, :] = next_ref[0, :, cols].astype(f32) * last
        w = w_ref[:, cols]

        for k in range(ts // sub):
            r0 = k * sub
            y = jnp.zeros((sub, LANES), f32)
            for j in range(A_CONV):
                off = r0 + HALO - pad + j
                y = y + a_ref[off:off + sub, :] * w[j:j + 1]
            y = y * _sigmoid(y)
            if c < 2 * A_WIDTH // LANES:
                ss = jnp.dot(y * y, head_ones, preferred_element_type=f32, precision=HIGHEST)
                qscale = HEAD_DIM ** -0.5 if c < A_WIDTH // LANES else 1.0
                y = y * (lax.rsqrt(ss + EPS) * qscale)
            o_ref[0, r0:r0 + sub, cols] = y.astype(o_ref.dtype)

    ch = CHUNK
    ri = lax.broadcasted_iota(i32, (ch, ch), 0)
    ci = lax.broadcasted_iota(i32, (ch, ch), 1)
    lower = (ri >= ci).astype(f32)
    upper = (ri <= ci).astype(f32)
    lane = lax.broadcasted_iota(i32, (ch, LANES), 1)
    g_lane = 2 * A_HEADS
    is_fwd = lane < g_lane + A_HEADS
    for k in range(ts // ch):
        rows = slice(k * ch, (k + 1) * ch)
        sm = sm_ref[0, rows, :]
        g = -par_ref[0:1, :] * _softplus(sm + par_ref[1:2, :])
        cum = jnp.where(is_fwd,
                        jnp.dot(lower, g, preferred_element_type=f32, precision=HIGHEST),
                        jnp.dot(upper, g, preferred_element_type=f32, precision=HIGHEST))
        cum = pltpu.roll(cum, CUM_LANE - g_lane, axis=1)
        gb_ref[0, rows, :] = jnp.where(lane < g_lane, _sigmoid(sm), cum)


def _prep(main3, small3, conv_w, par):
    b, s, _ = main3.shape
    ts = min(1024, s)
    n_tiles = s // ts
    hb = ts // HALO
    wa = 3 * A_WIDTH
    return pl.pallas_call(
        functools.partial(_prep_kernel, ts=ts, n_tiles=n_tiles),
        grid=(b, n_tiles),
        in_specs=[
            pl.BlockSpec((1, ts, wa), lambda bi, i: (bi, i, 0)),
            pl.BlockSpec((1, HALO, wa), lambda bi, i: (bi, jnp.maximum(i * hb - 1, 0), 0)),
            pl.BlockSpec((1, HALO, wa), lambda bi, i: (bi, jnp.minimum((i + 1) * hb, s // HALO - 1), 0)),
            pl.BlockSpec((A_CONV, wa), lambda bi, i: (0, 0)),
            pl.BlockSpec((1, ts, LANES), lambda bi, i: (bi, i, 0)),
            pl.BlockSpec((8, LANES), lambda bi, i: (0, 0)),
        ],
        out_specs=[pl.BlockSpec((1, ts, wa), lambda bi, i: (bi, i, 0)),
                   pl.BlockSpec((1, ts, LANES), lambda bi, i: (bi, i, 0))],
        out_shape=[jax.ShapeDtypeStruct((b, s, wa), bf16), jax.ShapeDtypeStruct((b, s, LANES), f32)],
        scratch_shapes=[pltpu.VMEM((ts + 2 * HALO, LANES), f32)],
        compiler_params=_cparams(("parallel", "parallel")),
        name="prep",
    )(main3, main3, main3, conv_w, small3, par)


GROUP = 8


def _split_heads(x, mask):
    z = jnp.zeros_like(x)
    return jnp.concatenate([jnp.where(mask, x, z), jnp.where(mask, z, x)], axis=0)


def _delta_stages(pair):
    c = CHUNK
    lane = lax.broadcasted_iota(i32, (c, LANES), 1)
    m0 = lane < HEAD_DIM
    m0w = jnp.concatenate([m0, m0], axis=1)
    rl = lax.broadcasted_iota(i32, (c, LANES), 0)
    cl = lane % HEAD_DIM
    eye_p = (rl == cl).astype(f32)
    same_head = (lax.broadcasted_iota(i32, (LANES, LANES), 0) // HEAD_DIM
                 == lax.broadcasted_iota(i32, (LANES, LANES), 1) // HEAD_DIM)
    bd = lambda x: _split_heads(x, m0).astype(bf16)
    bdw = lambda x: _split_heads(x, m0w).astype(bf16)

    def pick(full, base):
        c0 = jnp.sum(jnp.where(lane == base + 2 * pair, full, 0.0), axis=1, keepdims=True)
        c1 = jnp.sum(jnp.where(lane == base + 2 * pair + 1, full, 0.0), axis=1, keepdims=True)
        return jnp.where(m0, c0, c1)

    def gates(st):
        d = 1 if st["upper"] else 0
        st["beta"] = pick(st["gb"], BETA_LANE + d * A_HEADS)
        st["cum"] = pick(st["gb"], CUM_LANE + d * A_HEADS)

    def gram(st):
        q, k = st["q"], st["k"]
        st["gq"] = lax.dot_general(jnp.concatenate([q, k], axis=0), _split_heads(k, m0), (((1,), (1,)), ((), ())),
                                   preferred_element_type=f32)

    def decay(st):
        cum = st["cum"]
        cum_t = cum.T
        cum_row = jnp.concatenate([cum_t[0:1], cum_t[HEAD_DIM:HEAD_DIM + 1]], axis=1)
        incl = (rl <= cl) if st["upper"] else (rl >= cl)
        strict = (rl < cl) if st["upper"] else (rl > cl)
        dec = jnp.where(incl, jnp.exp(jnp.minimum(cum - cum_row, 0.0)), 0.0)
        gq = st.pop("gq")
        st["attn"] = (gq[0:c] * dec).astype(bf16)
        a = jnp.where(strict, -(st["beta"] * gq[c:2 * c] * dec), 0.0)
        st["p"] = eye_p + a
        st["a"] = jnp.dot(a.astype(bf16), bd(a), preferred_element_type=f32)

    def double(st):
        a, p = st["a"], st["p"]
        y = jnp.dot(jnp.concatenate([a, p], axis=0).astype(bf16), bd(a), preferred_element_type=f32)
        st["a"] = y[0:c]
        st["p"] = p + y[c:2 * c]

    def solve(st):
        a, p = st.pop("a"), st.pop("p")
        tinv = (p + jnp.dot(p.astype(bf16), bd(a), preferred_element_type=f32)).astype(bf16)
        cum = st["cum"]
        st["gtot"] = cum[0:1] if st["upper"] else cum[c - 1:c]
        st["eg"] = jnp.exp(cum)
        kf = st["k"].astype(f32)
        rhs = jnp.concatenate([st["v"].astype(f32) * st["beta"], kf * st["beta"] * st["eg"]], axis=1)
        st["uw"] = jnp.dot(tinv, bdw(rhs), preferred_element_type=f32)

    def finish(st):
        uw = st.pop("uw")
        aw = jnp.dot(st.pop("attn"), bdw(uw), preferred_element_type=f32)
        kd = (st["k"].astype(f32) * jnp.exp(st["gtot"] - st["cum"])).astype(bf16)
        t = lax.dot_general(kd, uw.astype(bf16), (((0,), (0,)), ((), ())), preferred_element_type=f32)
        st["oin"] = aw[:, 0:LANES]
        st["qeff"] = (st["q"].astype(f32) * st["eg"] - aw[:, LANES:2 * LANES]).astype(bf16)
        st["bbd"] = jnp.where(same_head, t[:, 0:LANES], 0.0)
        st["abd"] = jnp.where(same_head, -t[:, LANES:2 * LANES], 0.0).astype(bf16)
        st["dec"] = jnp.exp(st["gtot"])

    n_double = int(np.log2(c)) - 2
    return [gates, gram, decay] + [double] * n_double + [solve, finish]


def _delta_kernel(q_ref, k_ref, v_ref, z_ref, gb_ref, an_ref, o_ref,
                  acc_ref, st_ref, qe_ref, oi_ref, ab_ref, bb_ref, dc_ref, *, s):
    c = CHUNK
    n = s // c
    g = min(GROUP, n)
    ng = n // g
    pair = pl.program_id(1)
    acc_ref[...] = jnp.zeros_like(acc_ref)
    st_ref[...] = jnp.zeros_like(st_ref)
    stages = _delta_stages(pair)

    def row_start(gi, t, d):
        cidx = gi * g + t if d == 0 else n - 1 - (gi * g + t)
        return pl.multiple_of(cidx * c, c)

    def group_step(gi_a, slot_a, gi_b, slot_b):
        streams = []
        if gi_a is not None:
            for t in range(g):
                for d in range(2):
                    r0 = row_start(gi_a, t, d)
                    streams.append(dict(
                        q=q_ref[0, pl.ds(r0, c), :], k=k_ref[0, pl.ds(r0, c), :], v=v_ref[0, pl.ds(r0, c), :],
                        gb=gb_ref[0, pl.ds(r0, c), :], upper=(d == 1), idx=(d * 2 + slot_a) * g + t))
        states = [st_ref[0], st_ref[1]] if gi_b is not None else None

        def recurrence(t):
            for d in range(2):
                r0 = row_start(gi_b, t, d)
                idx = (d * 2 + slot_b) * g + t
                y = jnp.dot(jnp.concatenate([ab_ref[idx], qe_ref[idx]], axis=0), states[d].astype(bf16),
                            preferred_element_type=f32)
                acc_ref[pl.ds(r0, c), :] += y[LANES:LANES + c] + oi_ref[idx]
                states[d] = states[d] * dc_ref[idx][0:1] + y[0:LANES] + bb_ref[idx]

        done_b = 0
        for si, stage in enumerate(stages):
            for st in streams:
                stage(st)
            if gi_b is not None and si >= 1 and done_b < g:
                recurrence(done_b)
                done_b += 1
        if gi_b is not None:
            for t in range(done_b, g):
                recurrence(t)
            st_ref[0] = states[0]
            st_ref[1] = states[1]
        for st in streams:
            idx = st["idx"]
            qe_ref[idx] = st["qeff"]
            oi_ref[idx] = st["oin"]
            ab_ref[idx] = st["abd"]
            bb_ref[idx] = st["bbd"]
            dc_ref[idx] = jnp.broadcast_to(st["dec"], (8, LANES))

    group_step(0, 0, None, None)

    def body(i, carry):
        group_step(i, i % 2, i - 1, (i - 1) % 2)
        return carry

    lax.fori_loop(1, ng, body, 0)
    group_step(None, None, ng - 1, (ng - 1) % 2)

    rows = min(256, s)
    r = lax.broadcasted_iota(i32, (LANES, LANES), 0) // HEAD_DIM
    cc = lax.broadcasted_iota(i32, (LANES, LANES), 1) // HEAD_DIM
    head_mean = (r == cc).astype(f32) * (1.0 / HEAD_DIM)

    def epi(i, carry):
        r0 = pl.multiple_of(i * rows, rows)
        o = acc_ref[pl.ds(r0, rows), :]
        ms = jnp.dot(o * o, head_mean, preferred_element_type=f32, precision=HIGHEST)
        z = z_ref[0, pl.ds(r0, rows), :].astype(f32)
        y = o * lax.rsqrt(ms + EPS) * an_ref[...] * (z * _sigmoid(z))
        o_ref[0, pl.ds(r0, rows), :] = y.astype(o_ref.dtype)
        return carry

    lax.fori_loop(0, s // rows, epi, 0)


def _delta(qkvn, main3, gb3, an):
    b, s, _ = qkvn.shape
    npair = A_HEADS // 2
    kb = A_WIDTH // LANES
    nbuf = 4 * min(GROUP, s // CHUNK)
    seq = lambda off: pl.BlockSpec((1, s, LANES), lambda bi, p: (bi, 0, off + p))
    return pl.pallas_call(
        functools.partial(_delta_kernel, s=s),
        grid=(b, npair),
        in_specs=[
            seq(0), seq(kb), seq(2 * kb), seq(Z_BLK),
            pl.BlockSpec((1, s, LANES), lambda bi, p: (bi, 0, 0)),
            pl.BlockSpec((1, LANES), lambda bi, p: (0, 0)),
        ],
        out_specs=pl.BlockSpec((1, s, LANES), lambda bi, p: (bi, 0, p)),
        out_shape=jax.ShapeDtypeStruct((b, s, A_WIDTH), bf16),
        scratch_shapes=[pltpu.VMEM((s, LANES), f32), pltpu.VMEM((2, LANES, LANES), f32),
                        pltpu.VMEM((nbuf, CHUNK, LANES), bf16), pltpu.VMEM((nbuf, CHUNK, LANES), f32),
                        pltpu.VMEM((nbuf, LANES, LANES), bf16), pltpu.VMEM((nbuf, LANES, LANES), f32),
                        pltpu.VMEM((nbuf, 8, LANES), f32)],
        compiler_params=_cparams(("parallel", "arbitrary")),
        name="delta",
    )(qkvn, qkvn, qkvn, main3, gb3, an)


def _attn_kernel(q_ref, k_ref, v_ref, o_ref, l_ref, bias_ref, *, lp, bq, nk, dil, slopes):
    w = B_GROUP_WIDTH
    nh = B_HEADS_PER_GROUP
    head_of_lane = lax.broadcasted_iota(i32, (1, w), 1) // HEAD_DIM
    nq = lp // bq
    row = lax.broadcasted_iota(i32, (bq, nk), 0)
    col = lax.broadcasted_iota(i32, (bq, nk), 1)

    def alibi_bias(offset):
        adelta = jnp.abs(col - row + offset)
        dist = adelta.astype(f32) * float(dil)
        return [jnp.where(adelta <= B_SIDE, -slopes[h] * dist, NEG) for h in range(nh)]

    interior = -B_SIDE
    has_interior = nq > 2 and nk == bq + 2 * B_SIDE
    if has_interior:
        for h, b in enumerate(alibi_bias(interior)):
            bias_ref[h] = b

    def blocks(idxs, hoisted):
        work = []
        for i in idxs:
            q0 = pl.multiple_of(i * bq, bq)
            ks = pl.multiple_of(jnp.clip(q0 - B_SIDE, 0, lp - nk), B_SIDE)
            q = q_ref[0, pl.ds(q0, bq), :]
            zq = jnp.zeros_like(q)
            qs = jnp.concatenate([jnp.where(head_of_lane == h, q, zq) for h in range(nh)], axis=0)
            sc = lax.dot_general(qs, k_ref[0, pl.ds(ks, nk), :], (((1,), (1,)), ((), ())),
                                 preferred_element_type=f32)
            bias = [bias_ref[h] for h in range(nh)] if hoisted else alibi_bias(ks - q0)
            work.append(dict(q0=q0, ks=ks, sc=sc, bias=bias, o=jnp.zeros((bq, w), f32), lse=jnp.zeros((bq, w), f32)))
        for h in range(nh):
            hm = head_of_lane == h
            for wk in work:
                sh = wk["sc"][h * bq:(h + 1) * bq] + wk["bias"][h]
                m = jnp.max(sh, axis=1, keepdims=True)
                p = jnp.exp(sh - m)
                l = jnp.sum(p, axis=1, keepdims=True)
                oh = jnp.dot(p.astype(bf16), v_ref[0, pl.ds(wk["ks"], nk), :], preferred_element_type=f32) / l
                wk["o"] = jnp.where(hm, oh, wk["o"])
                wk["lse"] = jnp.where(hm, m + jnp.log(l), wk["lse"])
        for wk in work:
            o_ref[0, pl.ds(wk["q0"], bq), :] = wk["o"].astype(o_ref.dtype)
            l_ref[0, pl.ds(wk["q0"], bq), :] = wk["lse"]

    if nq % 2 == 1:
        blocks([0], False)
        assert nq == 1
        return
    npair = nq // 2

    def body(pi, carry):
        idxs = [2 * pi, 2 * pi + 1]
        if has_interior:
            is_interior = (pi > 0) & (pi < npair - 1)

            @pl.when(is_interior)
            def _():
                blocks(idxs, True)

            @pl.when(jnp.logical_not(is_interior))
            def _():
                blocks(idxs, False)
        else:
            blocks(idxs, False)
        return carry

    lax.fori_loop(0, npair, body, 0)


def _attention(main3, group):
    b, s, _ = main3.shape
    _, dil = B_GROUPS[group]
    lp = s // dil
    bq = min(128, lp)
    nk = min(bq + 2 * B_SIDE, lp)
    slopes = tuple(float(2.0 ** (-8.0 * (group * B_HEADS_PER_GROUP + h + 1) / B_HEADS))
                   for h in range(B_HEADS_PER_GROUP))
    w = B_GROUP_WIDTH
    ngrp = len(B_GROUPS)
    blk = lambda off: QKV_B_BLK256 + off * ngrp + group
    if dil == 1:
        src = main3
        spec = lambda off: pl.BlockSpec((1, lp, w), lambda bi, r: (bi, 0, blk(off)))
    else:
        qkv = jnp.concatenate([main3[:, :, blk(off) * w:(blk(off) + 1) * w] for off in range(3)], axis=-1)
        src = qkv.reshape(b, lp, dil * 3 * w)
        spec = lambda off: pl.BlockSpec((1, lp, w), lambda bi, r: (bi, 0, r * 3 + off))
    o, lse = pl.pallas_call(
        functools.partial(_attn_kernel, lp=lp, bq=bq, nk=nk, dil=dil, slopes=slopes),
        grid=(b, dil),
        in_specs=[spec(0), spec(1), spec(2)],
        out_specs=[pl.BlockSpec((1, lp, w), lambda bi, r: (bi, 0, r)),
                   pl.BlockSpec((1, lp, w), lambda bi, r: (bi, 0, r))],
        out_shape=[jax.ShapeDtypeStruct((b, lp, dil * w), bf16), jax.ShapeDtypeStruct((b, lp, dil * w), f32)],
        scratch_shapes=[pltpu.VMEM((B_HEADS_PER_GROUP, bq, nk), f32)],
        compiler_params=_cparams(("parallel", "parallel")),
        name=f"attn_d{dil}",
    )(src, src, src)
    return o.reshape(b * s, w), lse.reshape(b * s, w)


def _merge_kernel(x_ref, oa_ref, o1_ref, o2_ref, o3_ref, l1_ref, l2_ref, l3_ref, ga_ref, gb_ref,
                  wpa_ref, wpb_ref, wout_ref, g_ref, wr_ref, out_ref, hn_ref, aff_ref, afft_ref):
    l1, l2, l3 = l1_ref[...], l2_ref[...], l3_ref[...]
    m = jnp.maximum(jnp.maximum(l1, l2), l3)
    e1, e2, e3 = jnp.exp(l1 - m), jnp.exp(l2 - m), jnp.exp(l3 - m)
    ob = (e1 * o1_ref[...].astype(f32) + e2 * o2_ref[...].astype(f32) + e3 * o3_ref[...].astype(f32)) / (e1 + e2 + e3)
    ya = jnp.dot(oa_ref[...], wpa_ref[...], preferred_element_type=f32)
    yb = jnp.dot(ob.astype(bf16), wpb_ref[...], preferred_element_type=f32)
    mix = _sigmoid(ga_ref[...].astype(f32)) * ya + _sigmoid(gb_ref[...].astype(f32)) * yb
    x = x_ref[...] + jnp.dot(mix.astype(bf16), wout_ref[...], preferred_element_type=f32)
    out_ref[...] = x

    ms = jnp.mean(x * x, axis=-1, keepdims=True)
    hn = x * lax.rsqrt(ms + EPS) * g_ref[...]
    hn_ref[...] = hn.astype(bf16)
    hi = hn.astype(bf16)
    lo = (hn - hi.astype(f32)).astype(bf16)
    tm = hn.shape[0]
    prod = jnp.dot(jnp.concatenate([hi, lo], axis=0), wr_ref[...], preferred_element_type=f32)
    logits = (prod[0:tm, 0:LANES] + prod[0:tm, LANES:2 * LANES]) + prod[tm:2 * tm, 0:LANES]
    lane = lax.broadcasted_iota(i32, logits.shape, 1)
    logits = jnp.where(lane < N_EXPERTS, logits, NEG)
    mx = jnp.max(logits, axis=1, keepdims=True)
    e = jnp.exp(logits - mx)
    aff = e / jnp.sum(e, axis=1, keepdims=True)
    aff_ref[...] = aff
    afft_ref[...] = aff.T[0:N_EXPERTS]


def _merge_router(x2, oa2, obs, lses, main2, wpa, wpb, wout, g_ffn, wr):
    t = x2.shape[0]
    tm = min(512, t)
    w = B_GROUP_WIDTH
    row = lambda width: pl.BlockSpec((tm, width), lambda i: (i, 0))
    full = lambda a, bb: pl.BlockSpec((a, bb), lambda i: (0, 0))
    return pl.pallas_call(
        _merge_kernel,
        grid=(t // tm,),
        in_specs=[row(D_MODEL), row(A_WIDTH), row(w), row(w), row(w), row(w), row(w), row(w),
                  pl.BlockSpec((tm, D_MODEL), lambda i: (i, GATE_BLK)),
                  pl.BlockSpec((tm, D_MODEL), lambda i: (i, GATE_BLK + 1)),
                  full(A_WIDTH, D_MODEL), full(w, D_MODEL), full(D_MODEL, D_MODEL),
                  full(1, D_MODEL), full(D_MODEL, 2 * LANES)],
        out_specs=[row(D_MODEL), row(D_MODEL), row(LANES), pl.BlockSpec((N_EXPERTS, tm), lambda i: (0, i))],
        out_shape=[jax.ShapeDtypeStruct((t, D_MODEL), f32), jax.ShapeDtypeStruct((t, D_MODEL), bf16),
                   jax.ShapeDtypeStruct((t, LANES), f32), jax.ShapeDtypeStruct((N_EXPERTS, t), f32)],
        compiler_params=_cparams(("parallel",)),
        name="merge_router",
    )(x2, oa2, *obs, *lses, main2, main2, wpa, wpb, wout, g_ffn, wr)


def _threshold_kernel(afft_ref, thr_ref, need_ref, *, cap):
    bits = lax.bitcast_convert_type(afft_ref[...], i32)

    def step(i, lo):
        cand = lo | lax.shift_left(jnp.int32(1), 30 - i)
        cnt = jnp.sum((bits >= cand).astype(i32), axis=1, keepdims=True)
        return jnp.where(cnt >= cap, cand, lo)

    thr = lax.fori_loop(0, 31, step, jnp.zeros((N_EXPERTS, 1), i32))
    n_gt = jnp.sum((bits > thr).astype(i32), axis=1, keepdims=True)
    thr_ref[...] = jnp.broadcast_to(thr, (N_EXPERTS, LANES))
    need_ref[...] = jnp.broadcast_to(cap - n_gt, (N_EXPERTS, LANES))


def _threshold(afft, cap):
    return pl.pallas_call(
        functools.partial(_threshold_kernel, cap=cap),
        out_shape=[jax.ShapeDtypeStruct((N_EXPERTS, LANES), i32), jax.ShapeDtypeStruct((N_EXPERTS, LANES), i32)],
        compiler_params=pltpu.CompilerParams(vmem_limit_bytes=VMEM_LIMIT),
        name="threshold",
    )(afft)


def _select_kernel(aff_ref, thr_ref, need_ref, rk_ref, tb_ref, carry_ref):
    tt = SEL_TILE

    @pl.when(pl.program_id(0) == 0)
    def _():
        carry_ref[...] = jnp.zeros_like(carry_ref)

    bits = lax.bitcast_convert_type(aff_ref[...], i32)
    lane_ok = lax.broadcasted_iota(i32, (tt, LANES), 1) < N_EXPERTS
    thr = thr_ref[...]
    gt = (bits > thr) & lane_ok
    eq = (bits == thr) & lane_ok
    below = (lax.broadcasted_iota(i32, (tt, tt), 0) > lax.broadcasted_iota(i32, (tt, tt), 1)).astype(bf16)
    tie_carry = carry_ref[0:1, :]
    base = carry_ref[1:2, :]
    eqf = eq.astype(f32)
    tie_before = jnp.dot(below, eq.astype(bf16), preferred_element_type=f32) + tie_carry
    sel = gt | (eq & (tie_before < need_ref[...].astype(f32)))
    self_ = sel.astype(f32)
    rank = jnp.dot(below, sel.astype(bf16), preferred_element_type=f32)
    n = jnp.sum(self_, axis=0, keepdims=True)
    rk_ref[...] = jnp.where(sel, rank, -1.0)
    tb_ref[0, 0:1, :] = base.astype(i32)
    tb_ref[0, 1:2, :] = n.astype(i32)
    tb_ref[0, 2:8, :] = jnp.zeros((6, LANES), i32)
    carry_ref[0:1, :] = tie_carry + jnp.sum(eqf, axis=0, keepdims=True)
    carry_ref[1:2, :] = base + n


def _select(aff, thr_row, need_row):
    t = aff.shape[0]
    nt = t // SEL_TILE
    return pl.pallas_call(
        _select_kernel,
        grid=(nt,),
        in_specs=[pl.BlockSpec((SEL_TILE, LANES), lambda j: (j, 0)),
                  pl.BlockSpec((1, LANES), lambda j: (0, 0)),
                  pl.BlockSpec((1, LANES), lambda j: (0, 0))],
        out_specs=[pl.BlockSpec((SEL_TILE, LANES), lambda j: (j, 0)),
                   pl.BlockSpec((1, 8, LANES), lambda j: (j, 0, 0))],
        out_shape=[jax.ShapeDtypeStruct((t, LANES), f32), jax.ShapeDtypeStruct((nt, 8, LANES), i32)],
        scratch_shapes=[pltpu.VMEM((8, LANES), f32)],
        compiler_params=_cparams(("arbitrary",)),
        name="select",
    )(aff, thr_row, need_row)


WIN = SMALL_BUCKET + ROW_ALIGN
BIG_WIN = SEL_TILE + ROW_ALIGN


def _buckets(n):
    return ((WIN, (n > 0) & (n <= SMALL_BUCKET)), (BIG_WIN, n > SMALL_BUCKET))


def _aligned(x):
    return pl.multiple_of(x - x % ROW_ALIGN, ROW_ALIGN)


def _dispatch_kernel(base_ref, cnt_ref, fast_ref, rk_ref, hn_ref, xe_ref,
                     stage_ref, big_ref, carry_ref, sem_ref, bsem_ref, *, cap):
    j = pl.program_id(0)
    nt = pl.num_programs(0)
    tt = SEL_TILE
    ne = N_EXPERTS
    slot = j % 2

    @pl.when(j == 0)
    def _():
        carry_ref[...] = jnp.zeros_like(carry_ref)
        big_ref[0] = jnp.zeros((BIG_WIN, D_MODEL), bf16)
        tail = lambda e: pltpu.make_async_copy(big_ref.at[0], xe_ref.at[e, pl.ds(cap, BIG_WIN)], bsem_ref.at[e])
        for e in range(ne):
            tail(e).start()
        for e in range(ne):
            tail(e).wait()

    rkt = rk_ref[...].T
    hn = hn_ref[...]

    def targets(e):
        rem = base_ref[j * ne + e] % ROW_ALIGN
        row = rkt[e:e + 1, :]
        return jnp.where(row >= 0.0, row + rem.astype(f32), -1.0)

    def merge_carry(ref_rows, e):
        n = cnt_ref[j * ne + e]
        rem = base_ref[j * ne + e] % ROW_ALIGN
        head = ref_rows(0, ROW_ALIGN)
        head[...] += carry_ref[e]
        keep = pl.multiple_of(((rem + n) // ROW_ALIGN) * ROW_ALIGN, ROW_ALIGN)
        carry_ref[e] = ref_rows(keep, ROW_ALIGN)[...]

    def fast_copy(jj, sl, e):
        return pltpu.make_async_copy(stage_ref.at[sl, pl.ds(e * WIN, WIN)],
                                     xe_ref.at[e, pl.ds(_aligned(base_ref[jj * ne + e]), WIN)], sem_ref.at[e])

    def wait_previous():
        jp = jnp.maximum(j - 1, 0)

        @pl.when((j > 0) & (fast_ref[jp] == 1))
        def _():
            for e in range(ne):
                @pl.when(cnt_ref[jp * ne + e] > 0)
                def _(e=e):
                    fast_copy(jp, 1 - slot, e).wait()

    @pl.when(fast_ref[j] == 1)
    def _():
        win_slot = lax.broadcasted_iota(i32, (WIN, tt), 0).astype(f32)
        onehot = jnp.concatenate([(win_slot == targets(e)).astype(bf16) for e in range(ne)], axis=0)
        stage_ref[slot] = jnp.dot(onehot, hn, preferred_element_type=f32).astype(bf16)
        for e in range(ne):
            @pl.when(cnt_ref[j * ne + e] > 0)
            def _(e=e):
                merge_carry(lambda st, sz: stage_ref.at[slot, pl.ds(e * WIN + st, sz)], e)
        wait_previous()
        for e in range(ne):
            @pl.when(cnt_ref[j * ne + e] > 0)
            def _(e=e):
                fast_copy(j, slot, e).start()

        @pl.when(j == nt - 1)
        def _():
            for e in range(ne):
                @pl.when(cnt_ref[j * ne + e] > 0)
                def _(e=e):
                    fast_copy(j, slot, e).wait()

    @pl.when(fast_ref[j] == 0)
    def _():
        wait_previous()
        big_copy = lambda e, rows: pltpu.make_async_copy(
            big_ref.at[e, pl.ds(0, rows)], xe_ref.at[e, pl.ds(_aligned(base_ref[j * ne + e]), rows)], bsem_ref.at[e])
        for e in range(ne):
            for rows, cond in _buckets(cnt_ref[j * ne + e]):
                @pl.when(cond)
                def _(rows=rows, e=e):
                    win_slot = lax.broadcasted_iota(i32, (rows, tt), 0).astype(f32)
                    onehot = (win_slot == targets(e)).astype(bf16)
                    big_ref[e, 0:rows, :] = jnp.dot(onehot, hn, preferred_element_type=f32).astype(bf16)
                    merge_carry(lambda st, sz: big_ref.at[e, pl.ds(st, sz)], e)
                    big_copy(e, rows).start()
        for e in range(ne):
            for rows, cond in _buckets(cnt_ref[j * ne + e]):
                @pl.when(cond)
                def _(rows=rows, e=e):
                    big_copy(e, rows).wait()


def _dispatch(base, cnt, fast, rk, hn, cap):
    t = hn.shape[0]
    nt = t // SEL_TILE
    return pl.pallas_call(
        functools.partial(_dispatch_kernel, cap=cap),
        grid_spec=pltpu.PrefetchScalarGridSpec(
            num_scalar_prefetch=3,
            grid=(nt,),
            in_specs=[pl.BlockSpec((SEL_TILE, LANES), lambda j, b, c, f: (j, 0)),
                      pl.BlockSpec((SEL_TILE, D_MODEL), lambda j, b, c, f: (j, 0))],
            out_specs=pl.BlockSpec(memory_space=pl.ANY),
            scratch_shapes=[pltpu.VMEM((2, N_EXPERTS * WIN, D_MODEL), bf16),
                            pltpu.VMEM((N_EXPERTS, BIG_WIN, D_MODEL), bf16),
                            pltpu.VMEM((N_EXPERTS, ROW_ALIGN, D_MODEL), bf16),
                            pltpu.SemaphoreType.DMA((N_EXPERTS,)),
                            pltpu.SemaphoreType.DMA((N_EXPERTS,))],
        ),
        out_shape=jax.ShapeDtypeStruct((N_EXPERTS, cap + BIG_WIN, D_MODEL), bf16),
        compiler_params=_cparams(("arbitrary",)),
        name="dispatch",
    )(base, cnt, fast, rk, hn)


def _ffn_kernel(x_ref, wg_ref, wu_ref, wd_ref, y_ref):
    x = x_ref[0]
    g = jnp.dot(x, wg_ref[0], preferred_element_type=f32)
    u = jnp.dot(x, wu_ref[0], preferred_element_type=f32)
    h = (g * _sigmoid(g) * u).astype(bf16)
    y_ref[0] = jnp.dot(h, wd_ref[0], preferred_element_type=f32).astype(y_ref.dtype)


def _ffn(xe, wg, wu, wd, cap):
    tr = min(512, cap)
    wspec = lambda a, bb: pl.BlockSpec((1, a, bb), lambda e, i: (e, 0, 0))
    return pl.pallas_call(
        _ffn_kernel,
        grid=(N_EXPERTS, cap // tr),
        in_specs=[pl.BlockSpec((1, tr, D_MODEL), lambda e, i: (e, i, 0)),
                  wspec(D_MODEL, D_EXPERT), wspec(D_MODEL, D_EXPERT), wspec(D_EXPERT, D_MODEL)],
        out_specs=pl.BlockSpec((1, tr, D_MODEL), lambda e, i: (e, i, 0)),
        out_shape=jax.ShapeDtypeStruct((N_EXPERTS, cap, D_MODEL), bf16),
        compiler_params=_cparams(("parallel", "parallel")),
        name="expert_ffn",
    )(xe, wg, wu, wd)


def _combine_kernel(base_ref, cnt_ref, fast_ref, rk_ref, aff_ref, x_ref, gf_ref, ye_ref, out_ref,
                    buf_ref, big_ref, sem_ref, bsem_ref, *, cap, final_norm):
    j = pl.program_id(0)
    nt = pl.num_programs(0)
    tt = SEL_TILE
    ne = N_EXPERTS
    slot = j % 2
    win = min(WIN, cap)
    window_start = lambda base, rows: pl.multiple_of(jnp.minimum(base - base % ROW_ALIGN, cap - rows), ROW_ALIGN)

    def win_copy(jj, sl, e):
        return pltpu.make_async_copy(ye_ref.at[e, pl.ds(window_start(base_ref[jj * ne + e], win), win)],
                                     buf_ref.at[sl, pl.ds(e * win, win)], sem_ref.at[sl])

    @pl.when((j == 0) & (fast_ref[0] == 1))
    def _():
        for e in range(ne):
            win_copy(0, 0, e).start()

    jn = jnp.minimum(j + 1, nt - 1)

    @pl.when((j + 1 < nt) & (fast_ref[jn] == 1))
    def _():
        for e in range(ne):
            win_copy(jn, 1 - slot, e).start()

    @pl.when(fast_ref[j] == 1)
    def _():
        rkt = rk_ref[...].T
        afft = aff_ref[...].T
        win_slot = lax.broadcasted_iota(i32, (win, tt), 0).astype(f32)
        his, los = [], []
        for e in range(ne):
            base = base_ref[j * ne + e]
            shift = (base - window_start(base, win)).astype(f32)
            row = rkt[e:e + 1, :]
            gate = jnp.where((win_slot == row + shift) & (row >= 0.0), afft[e:e + 1, :], 0.0)
            hi = gate.astype(bf16)
            his.append(hi)
            los.append((gate - hi.astype(f32)).astype(bf16))
        lhs = jnp.concatenate([jnp.concatenate(his, axis=0), jnp.concatenate(los, axis=0)], axis=1)
        for e in range(ne):
            win_copy(j, slot, e).wait()
        y = lax.dot_general(lhs, buf_ref[slot], (((0,), (0,)), ((), ())), preferred_element_type=f32)
        out_ref[...] = x_ref[...] + y[0:tt] + y[tt:2 * tt]

    @pl.when(fast_ref[j] == 0)
    def _():
        buckets = lambda n: tuple((min(rows, cap), cond) for rows, cond in _buckets(n))
        big_copy = lambda e, rows: pltpu.make_async_copy(
            ye_ref.at[e, pl.ds(window_start(base_ref[j * ne + e], rows), rows)], big_ref.at[e, pl.ds(0, rows)],
            bsem_ref.at[e])
        for e in range(ne):
            for rows, cond in buckets(cnt_ref[j * ne + e]):
                @pl.when(cond)
                def _(rows=rows, e=e):
                    big_copy(e, rows).start()
        out_ref[...] = x_ref[...]
        for e in range(ne):
            for rows, cond in buckets(cnt_ref[j * ne + e]):
                @pl.when(cond)
                def _(rows=rows, e=e):
                    base = base_ref[j * ne + e]
                    big_copy(e, rows).wait()
                    col = rk_ref[:, e:e + 1]
                    tgt = col + (base - window_start(base, rows)).astype(f32)
                    win_slot = lax.broadcasted_iota(i32, (tt, rows), 1).astype(f32)
                    onehot = ((tgt == win_slot) & (col >= 0.0)).astype(bf16)
                    contrib = jnp.dot(onehot, big_ref[e, 0:rows, :], preferred_element_type=f32)
                    out_ref[...] += contrib * aff_ref[:, e:e + 1]

    if final_norm:
        y = out_ref[...]
        ms = jnp.mean(y * y, axis=-1, keepdims=True)
        out_ref[...] = y * lax.rsqrt(ms + EPS) * gf_ref[...]


def _combine(base, cnt, fast, rk, aff, x2, gfin, ye, cap, final_norm):
    t = x2.shape[0]
    nt = t // SEL_TILE
    tile = lambda width: pl.BlockSpec((SEL_TILE, width), lambda j, b, c, f: (j, 0))
    return pl.pallas_call(
        functools.partial(_combine_kernel, cap=cap, final_norm=final_norm),
        grid_spec=pltpu.PrefetchScalarGridSpec(
            num_scalar_prefetch=3,
            grid=(nt,),
            in_specs=[tile(LANES), tile(LANES), tile(D_MODEL),
                      pl.BlockSpec((1, D_MODEL), lambda j, b, c, f: (0, 0)),
                      pl.BlockSpec(memory_space=pl.ANY)],
            out_specs=tile(D_MODEL),
            scratch_shapes=[pltpu.VMEM((2, N_EXPERTS * min(WIN, cap), D_MODEL), bf16),
                            pltpu.VMEM((N_EXPERTS, min(BIG_WIN, cap), D_MODEL), bf16),
                            pltpu.SemaphoreType.DMA((2,)),
                            pltpu.SemaphoreType.DMA((N_EXPERTS,))],
        ),
        out_shape=jax.ShapeDtypeStruct((t, D_MODEL), f32),
        compiler_params=_cparams(("arbitrary",)),
        name="combine",
    )(base, cnt, fast, rk, aff, x2, gfin, ye)


def _split_bf16(w):
    hi = w.astype(bf16)
    return jnp.concatenate([hi, (w - hi.astype(f32)).astype(bf16)], axis=1)


def _prep_layer(l, norm_mix, w_in, conv_w, a_log, dt_bias, a_norm, w_proj_a, w_proj_b, w_out,
                norm_ffn, w_router, w_gate, w_up, w_down):
    w = w_in[l]
    small0 = 4 * A_WIDTH
    small1 = small0 + 4 * A_HEADS
    qkvb1 = small1 + 3 * B_HEADS * HEAD_DIM
    qb1 = small1 + B_HEADS * HEAD_DIM
    w_main = jnp.concatenate([w[:, :small0], w[:, qkvb1:], w[:, small1:qb1] * (HEAD_DIM ** -0.5), w[:, qb1:qkvb1]],
                             axis=1).astype(bf16)
    w_small = jnp.pad(w[:, small0:small1], ((0, 0), (0, LANES - 4 * A_HEADS))).astype(bf16)
    par = jnp.zeros((8, LANES), f32)
    par = par.at[0, 2 * A_HEADS:4 * A_HEADS].set(jnp.exp(a_log[l].astype(f32)).reshape(-1))
    par = par.at[1, 2 * A_HEADS:4 * A_HEADS].set(dt_bias[l].astype(f32).reshape(-1))
    return dict(
        g_mix=norm_mix[l].reshape(1, D_MODEL), w_main=w_main, w_small=w_small, conv_w=conv_w[l], par=par,
        an=jnp.tile(a_norm[l], 2).reshape(1, LANES),
        wpa=w_proj_a[l].astype(bf16), wpb=w_proj_b[l].astype(bf16), wout=w_out[l].astype(bf16),
        g_ffn=norm_ffn[l].reshape(1, D_MODEL),
        wr=_split_bf16(jnp.pad(w_router[l], ((0, 0), (0, LANES - N_EXPERTS)))),
        wg=w_gate[l].astype(bf16), wu=w_up[l].astype(bf16), wd=w_down[l].astype(bf16),
    )


def _layer(x2, b, s, lw, gfin, final_norm):
    t = b * s
    main2, small2 = _inproj(x2, lw["g_mix"], lw["w_main"], lw["w_small"])
    main3 = main2.reshape(b, s, MAIN_WIDTH)
    qkvn, gb3 = _prep(main3, small2.reshape(b, s, LANES), lw["conv_w"], lw["par"])
    oa = _delta(qkvn, main3, gb3, lw["an"])
    obs, lses = [], []
    for g in range(len(B_GROUPS)):
        o_g, l_g = _attention(main3, g)
        obs.append(o_g)
        lses.append(l_g)
    x2, hn, aff, afft = _merge_router(x2, oa.reshape(t, A_WIDTH), obs, lses, main2, lw["wpa"], lw["wpb"], lw["wout"],
                                      lw["g_ffn"], lw["wr"])
    cap = (CAPACITY_FACTOR * t) // N_EXPERTS
    thr, need = _threshold(afft, cap)
    pad_row = lambda a: jnp.pad(a[:, 0], (0, LANES - N_EXPERTS)).reshape(1, LANES)
    rk, tb = _select(aff, pad_row(thr), pad_row(need))
    base = tb[:, 0, :N_EXPERTS].reshape(-1)
    cnt = tb[:, 1, :N_EXPERTS].reshape(-1)
    fast = (jnp.max(tb[:, 1, :N_EXPERTS], axis=1) <= SMALL_BUCKET).astype(i32)
    xe = _dispatch(base, cnt, fast, rk, hn, cap)
    ye = _ffn(xe, lw["wg"], lw["wu"], lw["wd"], cap)
    return _combine(base, cnt, fast, rk, aff, x2, gfin, ye, cap, final_norm)


def kernel(x_prompt, x_sample, norm_mix, w_in, conv_w, a_log, dt_bias, a_norm, w_proj_a, w_proj_b, w_out,
           norm_ffn, w_router, w_gate, w_up, w_down, norm_final):
    depth = w_in.shape[0]
    layers = [_prep_layer(l, norm_mix, w_in, conv_w, a_log, dt_bias, a_norm, w_proj_a, w_proj_b, w_out,
                          norm_ffn, w_router, w_gate, w_up, w_down) for l in range(depth)]
    gfin = norm_final.reshape(1, D_MODEL)
    outs = []
    for x in (x_prompt, x_sample):
        b, s, d = x.shape
        x2 = x.reshape(b * s, d)
        for l in range(depth):
            x2 = _layer(x2, b, s, layers[l], gfin, l == depth - 1)
        outs.append(x2.reshape(b, s, d))
    return tuple(outs)
```

```python
import functools

import numpy as np
import jax
import jax.numpy as jnp
from jax import lax
from jax.experimental import pallas as pl
from jax.experimental.pallas import tpu as pltpu

f32 = jnp.float32
bf16 = jnp.bfloat16
i32 = jnp.int32
HIGHEST = lax.Precision.HIGHEST

D_MODEL = 1024
A_HEADS = 8
HEAD_DIM = 64
A_WIDTH = A_HEADS * HEAD_DIM
A_CONV = 5
CHUNK = 64
B_GROUPS = ((128, 1), (512, 4), (2048, 16))
B_HEADS_PER_GROUP = 4
B_HEADS = B_HEADS_PER_GROUP * len(B_GROUPS)
B_GROUP_WIDTH = B_HEADS_PER_GROUP * HEAD_DIM
B_SIDE = 64
N_EXPERTS = 16
D_EXPERT = 1024
CAPACITY_FACTOR = 2
EPS = 1e-6
NEG = -1e30

LANES = 128
MAIN_WIDTH = 2 * D_MODEL + 4 * A_WIDTH + 3 * B_HEADS * HEAD_DIM
Z_BLK = 3 * A_WIDTH // LANES
GATE_BLK = 4 * A_WIDTH // D_MODEL
QKV_B_BLK256 = (2 * D_MODEL + 4 * A_WIDTH) // B_GROUP_WIDTH
MAIN_BLK256 = MAIN_WIDTH // B_GROUP_WIDTH
N_TILE = 1280
SEL_TILE = 256
SMALL_BUCKET = 64
ROW_ALIGN = 16
VMEM_LIMIT = 56 * 1024 * 1024


def _cparams(sem):
    return pltpu.CompilerParams(dimension_semantics=sem, vmem_limit_bytes=VMEM_LIMIT)


def _sigmoid(x):
    return 1.0 / (1.0 + jnp.exp(-x))


def _softplus(x):
    return jnp.maximum(x, 0.0) + jnp.log(1.0 + jnp.exp(-jnp.abs(x)))


def _inproj_kernel(x_ref, g_ref, w_ref, ws_ref, o_ref, os_ref, n_ref):
    @pl.when(pl.program_id(1) == 0)
    def _():
        x = x_ref[...]
        ms = jnp.mean(x * x, axis=-1, keepdims=True)
        n = (x * lax.rsqrt(ms + EPS) * g_ref[...]).astype(bf16)
        n_ref[...] = n
        os_ref[...] = jnp.dot(n, ws_ref[...], preferred_element_type=f32)

    o_ref[...] = jnp.dot(n_ref[...], w_ref[...], preferred_element_type=f32).astype(o_ref.dtype)


def _inproj(x2, g, w_main, w_small):
    t = x2.shape[0]
    tm = min(1024, t)
    return pl.pallas_call(
        _inproj_kernel,
        grid=(t // tm, MAIN_WIDTH // N_TILE),
        in_specs=[
            pl.BlockSpec((tm, D_MODEL), lambda i, j: (i, 0)),
            pl.BlockSpec((1, D_MODEL), lambda i, j: (0, 0)),
            pl.BlockSpec((D_MODEL, N_TILE), lambda i, j: (0, j)),
            pl.BlockSpec((D_MODEL, LANES), lambda i, j: (0, 0)),
        ],
        out_specs=[
            pl.BlockSpec((tm, N_TILE), lambda i, j: (i, j)),
            pl.BlockSpec((tm, LANES), lambda i, j: (i, 0)),
        ],
        out_shape=[jax.ShapeDtypeStruct((t, MAIN_WIDTH), bf16), jax.ShapeDtypeStruct((t, LANES), f32)],
        scratch_shapes=[pltpu.VMEM((tm, D_MODEL), bf16)],
        compiler_params=_cparams(("parallel", "arbitrary")),
        name="inproj",
    )(x2, g, w_main, w_small)


HALO = 16
PREP_ROWS = 256
BETA_LANE = 0
CUM_LANE = 4 * A_HEADS


def _prep_kernel(cur_ref, prev_ref, next_ref, w_ref, sm_ref, par_ref, *rest, ts, n_tiles):
    dilated = [(g, dil) for g, (_, dil) in enumerate(B_GROUPS) if dil > 1]
    qkv_refs = rest[:3 * len(dilated)]
    o_ref, gb_ref = rest[3 * len(dilated):3 * len(dilated) + 2]
    ph_refs = rest[3 * len(dilated) + 2:3 * len(dilated) + 2 + len(dilated)]
    a_ref, s_ref = rest[-2:]
    i = pl.program_id(1)
    first = (i > 0).astype(f32)
    last = (i < n_tiles - 1).astype(f32)
    pad = (A_CONV - 1) // 2
    r = lax.broadcasted_iota(i32, (LANES, LANES), 0) // HEAD_DIM
    cc = lax.broadcasted_iota(i32, (LANES, LANES), 1) // HEAD_DIM
    head_ones = (r == cc).astype(f32)
    sub = min(PREP_ROWS, ts)
    for c in range(3 * A_WIDTH // LANES):
        cols = slice(c * LANES, (c + 1) * LANES)
        a_ref[0:HALO, :] = prev_ref[0, :, cols].astype(f32) * first
        a_ref[HALO:HALO + ts, :] = cur_ref[0, :, cols].astype(f32)
        a_ref[HALO + ts:2 * HALO + ts, :] = next_ref[0, :, cols].astype(f32) * last
        w = w_ref[:, cols]

        for k in range(ts // sub):
            r0 = k * sub
            y = jnp.zeros((sub, LANES), f32)
            for j in range(A_CONV):
                off = r0 + HALO - pad + j
                y = y + a_ref[off:off + sub, :] * w[j:j + 1]
            y = y * _sigmoid(y)
            if c < 2 * A_WIDTH // LANES:
                ss = jnp.dot(y * y, head_ones, preferred_element_type=f32, precision=HIGHEST)
                qscale = HEAD_DIM ** -0.5 if c < A_WIDTH // LANES else 1.0
                y = y * (lax.rsqrt(ss + EPS) * qscale)
            o_ref[0, r0:r0 + sub, cols] = y.astype(o_ref.dtype)

    ch = CHUNK
    ri = lax.broadcasted_iota(i32, (ch, ch), 0)
    ci = lax.broadcasted_iota(i32, (ch, ch), 1)
    lower = (ri >= ci).astype(f32)
    upper = (ri <= ci).astype(f32)
    lane = lax.broadcasted_iota(i32, (ch, LANES), 1)
    g_lane = 2 * A_HEADS
    is_fwd = lane < g_lane + A_HEADS
    for k in range(ts // ch):
        rows = slice(k * ch, (k + 1) * ch)
        sm = sm_ref[0, rows, :]
        g = -par_ref[0:1, :] * _softplus(sm + par_ref[1:2, :])
        cum = jnp.where(is_fwd,
                        jnp.dot(lower, g, preferred_element_type=f32, precision=HIGHEST),
                        jnp.dot(upper, g, preferred_element_type=f32, precision=HIGHEST))
        cum = pltpu.roll(cum, CUM_LANE - g_lane, axis=1)
        gb_ref[0, rows, :] = jnp.where(lane < g_lane, _sigmoid(sm), cum)

    w = B_GROUP_WIDTH
    for gi, (g, dil) in enumerate(dilated):
        for off in range(3):
            x = qkv_refs[3 * gi + off]
            for h in range(w // LANES):
                s_ref[h] = x[0, :, h * LANES:(h + 1) * LANES].astype(f32)
            for r in range(dil):
                for h in range(w // LANES):
                    c0 = (r * 3 + off) * w + h * LANES
                    ph_refs[gi][0, :, c0:c0 + LANES] = s_ref[h, pl.ds(r, ts // dil, stride=dil), :].astype(bf16)


def _prep(main3, small3, conv_w, par):
    b, s, _ = main3.shape
    ts = min(1024, s)
    n_tiles = s // ts
    hb = ts // HALO
    wa = 3 * A_WIDTH
    w = B_GROUP_WIDTH
    ngrp = len(B_GROUPS)
    dilated = [(g, dil) for g, (_, dil) in enumerate(B_GROUPS) if dil > 1]
    qkv_specs = [pl.BlockSpec((1, ts, w), functools.partial(lambda bi, i, blk: (bi, i, blk), blk=QKV_B_BLK256 + off * ngrp + g))
                 for g, _ in dilated for off in range(3)]
    outs = pl.pallas_call(
        functools.partial(_prep_kernel, ts=ts, n_tiles=n_tiles),
        grid=(b, n_tiles),
        in_specs=[
            pl.BlockSpec((1, ts, wa), lambda bi, i: (bi, i, 0)),
            pl.BlockSpec((1, HALO, wa), lambda bi, i: (bi, jnp.maximum(i * hb - 1, 0), 0)),
            pl.BlockSpec((1, HALO, wa), lambda bi, i: (bi, jnp.minimum((i + 1) * hb, s // HALO - 1), 0)),
            pl.BlockSpec((A_CONV, wa), lambda bi, i: (0, 0)),
            pl.BlockSpec((1, ts, LANES), lambda bi, i: (bi, i, 0)),
            pl.BlockSpec((8, LANES), lambda bi, i: (0, 0)),
        ] + qkv_specs,
        out_specs=[pl.BlockSpec((1, ts, wa), lambda bi, i: (bi, i, 0)),
                   pl.BlockSpec((1, ts, LANES), lambda bi, i: (bi, i, 0))]
        + [pl.BlockSpec((1, ts // dil, dil * 3 * w), lambda bi, i: (bi, i, 0)) for _, dil in dilated],
        out_shape=[jax.ShapeDtypeStruct((b, s, wa), bf16), jax.ShapeDtypeStruct((b, s, LANES), f32)]
        + [jax.ShapeDtypeStruct((b, s // dil, dil * 3 * w), bf16) for _, dil in dilated],
        scratch_shapes=[pltpu.VMEM((ts + 2 * HALO, LANES), f32), pltpu.VMEM((w // LANES, ts, LANES), f32)],
        compiler_params=_cparams(("parallel", "parallel")),
        name="prep",
    )(main3, main3, main3, conv_w, small3, par, *([main3] * len(qkv_specs)))
    return outs[0], outs[1], {g: ph for (g, _), ph in zip(dilated, outs[2:])}


GROUP = 8


def _split_heads(x, mask):
    z = jnp.zeros_like(x)
    return jnp.concatenate([jnp.where(mask, x, z), jnp.where(mask, z, x)], axis=0)


def _delta_stages(pair):
    c = CHUNK
    lane = lax.broadcasted_iota(i32, (c, LANES), 1)
    m0 = lane < HEAD_DIM
    m0w = jnp.concatenate([m0, m0], axis=1)
    rl = lax.broadcasted_iota(i32, (c, LANES), 0)
    cl = lane % HEAD_DIM
    eye_p = (rl == cl).astype(f32)
    same_head = (lax.broadcasted_iota(i32, (LANES, LANES), 0) // HEAD_DIM
                 == lax.broadcasted_iota(i32, (LANES, LANES), 1) // HEAD_DIM)
    bd = lambda x: _split_heads(x, m0).astype(bf16)
    bdw = lambda x: _split_heads(x, m0w).astype(bf16)

    def pick(full, base):
        c0 = jnp.sum(jnp.where(lane == base + 2 * pair, full, 0.0), axis=1, keepdims=True)
        c1 = jnp.sum(jnp.where(lane == base + 2 * pair + 1, full, 0.0), axis=1, keepdims=True)
        return jnp.where(m0, c0, c1)

    def gates(st):
        d = 1 if st["upper"] else 0
        st["beta"] = pick(st["gb"], BETA_LANE + d * A_HEADS)
        st["cum"] = pick(st["gb"], CUM_LANE + d * A_HEADS)

    def gram(st):
        q, k = st["q"], st["k"]
        st["gq"] = lax.dot_general(jnp.concatenate([q, k], axis=0), _split_heads(k, m0), (((1,), (1,)), ((), ())),
                                   preferred_element_type=f32)

    def decay(st):
        cum = st["cum"]
        cum_t = cum.T
        cum_row = jnp.concatenate([cum_t[0:1], cum_t[HEAD_DIM:HEAD_DIM + 1]], axis=1)
        incl = (rl <= cl) if st["upper"] else (rl >= cl)
        strict = (rl < cl) if st["upper"] else (rl > cl)
        dec = jnp.where(incl, jnp.exp(jnp.minimum(cum - cum_row, 0.0)), 0.0)
        gq = st.pop("gq")
        st["attn"] = (gq[0:c] * dec).astype(bf16)
        a = jnp.where(strict, -(st["beta"] * gq[c:2 * c] * dec), 0.0)
        st["p"] = eye_p + a
        st["a"] = jnp.dot(a.astype(bf16), bd(a), preferred_element_type=f32)

    def double(st):
        a, p = st["a"], st["p"]
        y = jnp.dot(jnp.concatenate([a, p], axis=0).astype(bf16), bd(a), preferred_element_type=f32)
        st["a"] = y[0:c]
        st["p"] = p + y[c:2 * c]

    def solve(st):
        a, p = st.pop("a"), st.pop("p")
        tinv = (p + jnp.dot(p.astype(bf16), bd(a), preferred_element_type=f32)).astype(bf16)
        cum = st["cum"]
        st["gtot"] = cum[0:1] if st["upper"] else cum[c - 1:c]
        st["eg"] = jnp.exp(cum)
        kf = st["k"].astype(f32)
        rhs = jnp.concatenate([st["v"].astype(f32) * st["beta"], kf * st["beta"] * st["eg"]], axis=1)
        st["uw"] = jnp.dot(tinv, bdw(rhs), preferred_element_type=f32)

    def finish(st):
        uw = st.pop("uw")
        aw = jnp.dot(st.pop("attn"), bdw(uw), preferred_element_type=f32)
        kd = (st["k"].astype(f32) * jnp.exp(st["gtot"] - st["cum"])).astype(bf16)
        t = lax.dot_general(kd, uw.astype(bf16), (((0,), (0,)), ((), ())), preferred_element_type=f32)
        st["oin"] = aw[:, 0:LANES]
        st["qeff"] = (st["q"].astype(f32) * st["eg"] - aw[:, LANES:2 * LANES]).astype(bf16)
        st["bbd"] = jnp.where(same_head, t[:, 0:LANES], 0.0)
        st["abd"] = jnp.where(same_head, -t[:, LANES:2 * LANES], 0.0).astype(bf16)
        st["dec"] = jnp.exp(st["gtot"])

    n_double = int(np.log2(c)) - 2
    return [gates, gram, decay] + [double] * n_double + [solve, finish]


def _delta_kernel(q_ref, k_ref, v_ref, z_ref, gb_ref, an_ref, o_ref,
                  acc_ref, st_ref, qe_ref, oi_ref, ab_ref, bb_ref, dc_ref, *, s):
    c = CHUNK
    n = s // c
    g = min(GROUP, n)
    ng = n // g
    pair = pl.program_id(1)
    acc_ref[...] = jnp.zeros_like(acc_ref)
    st_ref[...] = jnp.zeros_like(st_ref)
    stages = _delta_stages(pair)

    def row_start(gi, t, d):
        cidx = gi * g + t if d == 0 else n - 1 - (gi * g + t)
        return pl.multiple_of(cidx * c, c)

    def group_step(gi_a, slot_a, gi_b, slot_b):
        streams = []
        if gi_a is not None:
            for t in range(g):
                for d in range(2):
                    r0 = row_start(gi_a, t, d)
                    streams.append(dict(
                        q=q_ref[0, pl.ds(r0, c), :], k=k_ref[0, pl.ds(r0, c), :], v=v_ref[0, pl.ds(r0, c), :],
                        gb=gb_ref[0, pl.ds(r0, c), :], upper=(d == 1), idx=(d * 2 + slot_a) * g + t))
        states = [st_ref[0], st_ref[1]] if gi_b is not None else None

        def recurrence(t):
            for d in range(2):
                r0 = row_start(gi_b, t, d)
                idx = (d * 2 + slot_b) * g + t
                y = jnp.dot(jnp.concatenate([ab_ref[idx], qe_ref[idx]], axis=0), states[d].astype(bf16),
                            preferred_element_type=f32)
                acc_ref[pl.ds(r0, c), :] += y[LANES:LANES + c] + oi_ref[idx]
                states[d] = states[d] * dc_ref[idx][0:1] + y[0:LANES] + bb_ref[idx]

        done_b = 0
        for si, stage in enumerate(stages):
            for st in streams:
                stage(st)
            if gi_b is not None and si >= 1 and done_b < g:
                recurrence(done_b)
                done_b += 1
        if gi_b is not None:
            for t in range(done_b, g):
                recurrence(t)
            st_ref[0] = states[0]
            st_ref[1] = states[1]
        for st in streams:
            idx = st["idx"]
            qe_ref[idx] = st["qeff"]
            oi_ref[idx] = st["oin"]
            ab_ref[idx] = st["abd"]
            bb_ref[idx] = st["bbd"]
            dc_ref[idx] = jnp.broadcast_to(st["dec"], (8, LANES))

    group_step(0, 0, None, None)

    def body(i, carry):
        group_step(i, i % 2, i - 1, (i - 1) % 2)
        return carry

    lax.fori_loop(1, ng, body, 0)
    group_step(None, None, ng - 1, (ng - 1) % 2)

    rows = min(256, s)
    r = lax.broadcasted_iota(i32, (LANES, LANES), 0) // HEAD_DIM
    cc = lax.broadcasted_iota(i32, (LANES, LANES), 1) // HEAD_DIM
    head_mean = (r == cc).astype(f32) * (1.0 / HEAD_DIM)

    def epi(i, carry):
        r0 = pl.multiple_of(i * rows, rows)
        o = acc_ref[pl.ds(r0, rows), :]
        ms = jnp.dot(o * o, head_mean, preferred_element_type=f32, precision=HIGHEST)
        z = z_ref[0, pl.ds(r0, rows), :].astype(f32)
        y = o * lax.rsqrt(ms + EPS) * an_ref[...] * (z * _sigmoid(z))
        o_ref[0, pl.ds(r0, rows), :] = y.astype(o_ref.dtype)
        return carry

    lax.fori_loop(0, s // rows, epi, 0)


def _delta(qkvn, main3, gb3, an):
    b, s, _ = qkvn.shape
    npair = A_HEADS // 2
    kb = A_WIDTH // LANES
    nbuf = 4 * min(GROUP, s // CHUNK)
    seq = lambda off: pl.BlockSpec((1, s, LANES), lambda bi, p: (bi, 0, off + p))
    return pl.pallas_call(
        functools.partial(_delta_kernel, s=s),
        grid=(b, npair),
        in_specs=[
            seq(0), seq(kb), seq(2 * kb), seq(Z_BLK),
            pl.BlockSpec((1, s, LANES), lambda bi, p: (bi, 0, 0)),
            pl.BlockSpec((1, LANES), lambda bi, p: (0, 0)),
        ],
        out_specs=pl.BlockSpec((1, s, LANES), lambda bi, p: (bi, 0, p)),
        out_shape=jax.ShapeDtypeStruct((b, s, A_WIDTH), bf16),
        scratch_shapes=[pltpu.VMEM((s, LANES), f32), pltpu.VMEM((2, LANES, LANES), f32),
                        pltpu.VMEM((nbuf, CHUNK, LANES), bf16), pltpu.VMEM((nbuf, CHUNK, LANES), f32),
                        pltpu.VMEM((nbuf, LANES, LANES), bf16), pltpu.VMEM((nbuf, LANES, LANES), f32),
                        pltpu.VMEM((nbuf, 8, LANES), f32)],
        compiler_params=_cparams(("parallel", "arbitrary")),
        name="delta",
    )(qkvn, qkvn, qkvn, main3, gb3, an)


def _attn_kernel(q_ref, k_ref, v_ref, o_ref, l_ref, bias_ref, *, lp, bq, nk, dil, slopes):
    w = B_GROUP_WIDTH
    nh = B_HEADS_PER_GROUP
    head_of_lane = lax.broadcasted_iota(i32, (1, w), 1) // HEAD_DIM
    nq = lp // bq
    row = lax.broadcasted_iota(i32, (bq, nk), 0)
    col = lax.broadcasted_iota(i32, (bq, nk), 1)

    def alibi_bias(offset):
        adelta = jnp.abs(col - row + offset)
        dist = adelta.astype(f32) * float(dil)
        return [jnp.where(adelta <= B_SIDE, -slopes[h] * dist, NEG) for h in range(nh)]

    interior = -B_SIDE
    has_interior = nq > 2 and nk == bq + 2 * B_SIDE
    if has_interior:
        for h, b in enumerate(alibi_bias(interior)):
            bias_ref[h] = b

    def blocks(idxs, hoisted):
        work = []
        for i in idxs:
            q0 = pl.multiple_of(i * bq, bq)
            ks = pl.multiple_of(jnp.clip(q0 - B_SIDE, 0, lp - nk), B_SIDE)
            q = q_ref[0, pl.ds(q0, bq), :]
            zq = jnp.zeros_like(q)
            qs = jnp.concatenate([jnp.where(head_of_lane == h, q, zq) for h in range(nh)], axis=0)
            sc = lax.dot_general(qs, k_ref[0, pl.ds(ks, nk), :], (((1,), (1,)), ((), ())),
                                 preferred_element_type=f32)
            bias = [bias_ref[h] for h in range(nh)] if hoisted else alibi_bias(ks - q0)
            work.append(dict(q0=q0, ks=ks, sc=sc, bias=bias, o=jnp.zeros((bq, w), f32), lse=jnp.zeros((bq, w), f32)))
        for h in range(nh):
            hm = head_of_lane == h
            for wk in work:
                sh = wk["sc"][h * bq:(h + 1) * bq] + wk["bias"][h]
                m = jnp.max(sh, axis=1, keepdims=True)
                p = jnp.exp(sh - m)
                l = jnp.sum(p, axis=1, keepdims=True)
                oh = jnp.dot(p.astype(bf16), v_ref[0, pl.ds(wk["ks"], nk), :], preferred_element_type=f32) / l
                wk["o"] = jnp.where(hm, oh, wk["o"])
                wk["lse"] = jnp.where(hm, m + jnp.log(l), wk["lse"])
        for wk in work:
            o_ref[0, pl.ds(wk["q0"], bq), :] = wk["o"].astype(o_ref.dtype)
            l_ref[0, pl.ds(wk["q0"], bq), :] = wk["lse"]

    if nq % 2 == 1:
        blocks([0], False)
        assert nq == 1
        return
    npair = nq // 2

    def body(pi, carry):
        idxs = [2 * pi, 2 * pi + 1]
        if has_interior:
            is_interior = (pi > 0) & (pi < npair - 1)

            @pl.when(is_interior)
            def _():
                blocks(idxs, True)

            @pl.when(jnp.logical_not(is_interior))
            def _():
                blocks(idxs, False)
        else:
            blocks(idxs, False)
        return carry

    lax.fori_loop(0, npair, body, 0)


def _attention(main3, group, phase_view):
    b, s, _ = main3.shape
    _, dil = B_GROUPS[group]
    lp = s // dil
    bq = min(128, lp)
    nk = min(bq + 2 * B_SIDE, lp)
    slopes = tuple(float(2.0 ** (-8.0 * (group * B_HEADS_PER_GROUP + h + 1) / B_HEADS))
                   for h in range(B_HEADS_PER_GROUP))
    w = B_GROUP_WIDTH
    ngrp = len(B_GROUPS)
    if dil == 1:
        src = main3
        spec = lambda off: pl.BlockSpec((1, lp, w), lambda bi, r: (bi, 0, QKV_B_BLK256 + off * ngrp + group))
    else:
        src = phase_view
        spec = lambda off: pl.BlockSpec((1, lp, w), lambda bi, r: (bi, 0, r * 3 + off))
    o, lse = pl.pallas_call(
        functools.partial(_attn_kernel, lp=lp, bq=bq, nk=nk, dil=dil, slopes=slopes),
        grid=(b, dil),
        in_specs=[spec(0), spec(1), spec(2)],
        out_specs=[pl.BlockSpec((1, lp, w), lambda bi, r: (bi, 0, r)),
                   pl.BlockSpec((1, lp, w), lambda bi, r: (bi, 0, r))],
        out_shape=[jax.ShapeDtypeStruct((b, lp, dil * w), bf16), jax.ShapeDtypeStruct((b, lp, dil * w), f32)],
        scratch_shapes=[pltpu.VMEM((B_HEADS_PER_GROUP, bq, nk), f32)],
        compiler_params=_cparams(("parallel", "parallel")),
        name=f"attn_d{dil}",
    )(src, src, src)
    return o.reshape(b * s, w), lse.reshape(b * s, w)


def _merge_kernel(x_ref, oa_ref, o1_ref, o2_ref, o3_ref, l1_ref, l2_ref, l3_ref, ga_ref, gb_ref,
                  wpa_ref, wpb_ref, wout_ref, g_ref, wr_ref, out_ref, hn_ref, aff_ref, afft_ref):
    l1, l2, l3 = l1_ref[...], l2_ref[...], l3_ref[...]
    m = jnp.maximum(jnp.maximum(l1, l2), l3)
    e1, e2, e3 = jnp.exp(l1 - m), jnp.exp(l2 - m), jnp.exp(l3 - m)
    ob = (e1 * o1_ref[...].astype(f32) + e2 * o2_ref[...].astype(f32) + e3 * o3_ref[...].astype(f32)) / (e1 + e2 + e3)
    ya = jnp.dot(oa_ref[...], wpa_ref[...], preferred_element_type=f32)
    yb = jnp.dot(ob.astype(bf16), wpb_ref[...], preferred_element_type=f32)
    mix = _sigmoid(ga_ref[...].astype(f32)) * ya + _sigmoid(gb_ref[...].astype(f32)) * yb
    x = x_ref[...] + jnp.dot(mix.astype(bf16), wout_ref[...], preferred_element_type=f32)
    out_ref[...] = x

    ms = jnp.mean(x * x, axis=-1, keepdims=True)
    hn = x * lax.rsqrt(ms + EPS) * g_ref[...]
    hn_ref[...] = hn.astype(bf16)
    hi = hn.astype(bf16)
    lo = (hn - hi.astype(f32)).astype(bf16)
    tm = hn.shape[0]
    prod = jnp.dot(jnp.concatenate([hi, lo], axis=0), wr_ref[...], preferred_element_type=f32)
    logits = (prod[0:tm, 0:LANES] + prod[0:tm, LANES:2 * LANES]) + prod[tm:2 * tm, 0:LANES]
    lane = lax.broadcasted_iota(i32, logits.shape, 1)
    logits = jnp.where(lane < N_EXPERTS, logits, NEG)
    mx = jnp.max(logits, axis=1, keepdims=True)
    e = jnp.exp(logits - mx)
    aff = e / jnp.sum(e, axis=1, keepdims=True)
    aff_ref[...] = aff
    afft_ref[...] = aff.T[0:N_EXPERTS]


def _merge_router(x2, oa2, obs, lses, main2, wpa, wpb, wout, g_ffn, wr):
    t = x2.shape[0]
    tm = min(512, t)
    w = B_GROUP_WIDTH
    row = lambda width: pl.BlockSpec((tm, width), lambda i: (i, 0))
    full = lambda a, bb: pl.BlockSpec((a, bb), lambda i: (0, 0))
    return pl.pallas_call(
        _merge_kernel,
        grid=(t // tm,),
        in_specs=[row(D_MODEL), row(A_WIDTH), row(w), row(w), row(w), row(w), row(w), row(w),
                  pl.BlockSpec((tm, D_MODEL), lambda i: (i, GATE_BLK)),
                  pl.BlockSpec((tm, D_MODEL), lambda i: (i, GATE_BLK + 1)),
                  full(A_WIDTH, D_MODEL), full(w, D_MODEL), full(D_MODEL, D_MODEL),
                  full(1, D_MODEL), full(D_MODEL, 2 * LANES)],
        out_specs=[row(D_MODEL), row(D_MODEL), row(LANES), pl.BlockSpec((N_EXPERTS, tm), lambda i: (0, i))],
        out_shape=[jax.ShapeDtypeStruct((t, D_MODEL), f32), jax.ShapeDtypeStruct((t, D_MODEL), bf16),
                   jax.ShapeDtypeStruct((t, LANES), f32), jax.ShapeDtypeStruct((N_EXPERTS, t), f32)],
        compiler_params=_cparams(("parallel",)),
        name="merge_router",
    )(x2, oa2, *obs, *lses, main2, main2, wpa, wpb, wout, g_ffn, wr)


def _threshold_kernel(afft_ref, thr_ref, need_ref, *, cap):
    bits = lax.bitcast_convert_type(afft_ref[...], i32)

    def step(i, lo):
        cand = lo | lax.shift_left(jnp.int32(1), 30 - i)
        cnt = jnp.sum((bits >= cand).astype(i32), axis=1, keepdims=True)
        return jnp.where(cnt >= cap, cand, lo)

    thr = lax.fori_loop(0, 31, step, jnp.zeros((N_EXPERTS, 1), i32))
    n_gt = jnp.sum((bits > thr).astype(i32), axis=1, keepdims=True)
    thr_ref[...] = jnp.broadcast_to(thr, (N_EXPERTS, LANES))
    need_ref[...] = jnp.broadcast_to(cap - n_gt, (N_EXPERTS, LANES))


def _threshold(afft, cap):
    return pl.pallas_call(
        functools.partial(_threshold_kernel, cap=cap),
        out_shape=[jax.ShapeDtypeStruct((N_EXPERTS, LANES), i32), jax.ShapeDtypeStruct((N_EXPERTS, LANES), i32)],
        compiler_params=pltpu.CompilerParams(vmem_limit_bytes=VMEM_LIMIT),
        name="threshold",
    )(afft)


def _select_kernel(aff_ref, thr_ref, need_ref, rk_ref, tb_ref, carry_ref):
    tt = SEL_TILE

    @pl.when(pl.program_id(0) == 0)
    def _():
        carry_ref[...] = jnp.zeros_like(carry_ref)

    bits = lax.bitcast_convert_type(aff_ref[...], i32)
    lane_ok = lax.broadcasted_iota(i32, (tt, LANES), 1) < N_EXPERTS
    thr = thr_ref[...]
    gt = (bits > thr) & lane_ok
    eq = (bits == thr) & lane_ok
    below = (lax.broadcasted_iota(i32, (tt, tt), 0) > lax.broadcasted_iota(i32, (tt, tt), 1)).astype(bf16)
    tie_carry = carry_ref[0:1, :]
    base = carry_ref[1:2, :]
    eqf = eq.astype(f32)
    tie_before = jnp.dot(below, eq.astype(bf16), preferred_element_type=f32) + tie_carry
    sel = gt | (eq & (tie_before < need_ref[...].astype(f32)))
    self_ = sel.astype(f32)
    rank = jnp.dot(below, sel.astype(bf16), preferred_element_type=f32)
    n = jnp.sum(self_, axis=0, keepdims=True)
    rk_ref[...] = jnp.where(sel, rank, -1.0)
    tb_ref[0, 0:1, :] = base.astype(i32)
    tb_ref[0, 1:2, :] = n.astype(i32)
    tb_ref[0, 2:8, :] = jnp.zeros((6, LANES), i32)
    carry_ref[0:1, :] = tie_carry + jnp.sum(eqf, axis=0, keepdims=True)
    carry_ref[1:2, :] = base + n


def _select(aff, thr_row, need_row):
    t = aff.shape[0]
    nt = t // SEL_TILE
    return pl.pallas_call(
        _select_kernel,
        grid=(nt,),
        in_specs=[pl.BlockSpec((SEL_TILE, LANES), lambda j: (j, 0)),
                  pl.BlockSpec((1, LANES), lambda j: (0, 0)),
                  pl.BlockSpec((1, LANES), lambda j: (0, 0))],
        out_specs=[pl.BlockSpec((SEL_TILE, LANES), lambda j: (j, 0)),
                   pl.BlockSpec((1, 8, LANES), lambda j: (j, 0, 0))],
        out_shape=[jax.ShapeDtypeStruct((t, LANES), f32), jax.ShapeDtypeStruct((nt, 8, LANES), i32)],
        scratch_shapes=[pltpu.VMEM((8, LANES), f32)],
        compiler_params=_cparams(("arbitrary",)),
        name="select",
    )(aff, thr_row, need_row)


WIN = SMALL_BUCKET + ROW_ALIGN
BIG_WIN = SEL_TILE + ROW_ALIGN


def _buckets(n):
    return ((WIN, (n > 0) & (n <= SMALL_BUCKET)), (BIG_WIN, n > SMALL_BUCKET))


def _aligned(x):
    return pl.multiple_of(x - x % ROW_ALIGN, ROW_ALIGN)


def _dispatch_kernel(base_ref, cnt_ref, fast_ref, rk_ref, hn_ref, xe_ref,
                     stage_ref, big_ref, carry_ref, sem_ref, bsem_ref, *, cap):
    j = pl.program_id(0)
    nt = pl.num_programs(0)
    tt = SEL_TILE
    ne = N_EXPERTS
    slot = j % 2

    @pl.when(j == 0)
    def _():
        carry_ref[...] = jnp.zeros_like(carry_ref)
        big_ref[0] = jnp.zeros((BIG_WIN, D_MODEL), bf16)
        tail = lambda e: pltpu.make_async_copy(big_ref.at[0], xe_ref.at[e, pl.ds(cap, BIG_WIN)], bsem_ref.at[e])
        for e in range(ne):
            tail(e).start()
        for e in range(ne):
            tail(e).wait()

    rkt = rk_ref[...].T
    hn = hn_ref[...]

    def targets(e):
        rem = base_ref[j * ne + e] % ROW_ALIGN
        row = rkt[e:e + 1, :]
        return jnp.where(row >= 0.0, row + rem.astype(f32), -1.0)

    def merge_carry(ref_rows, e):
        n = cnt_ref[j * ne + e]
        rem = base_ref[j * ne + e] % ROW_ALIGN
        head = ref_rows(0, ROW_ALIGN)
        head[...] += carry_ref[e]
        keep = pl.multiple_of(((rem + n) // ROW_ALIGN) * ROW_ALIGN, ROW_ALIGN)
        carry_ref[e] = ref_rows(keep, ROW_ALIGN)[...]

    def fast_copy(jj, sl, e):
        return pltpu.make_async_copy(stage_ref.at[sl, pl.ds(e * WIN, WIN)],
                                     xe_ref.at[e, pl.ds(_aligned(base_ref[jj * ne + e]), WIN)], sem_ref.at[e])

    def wait_previous():
        jp = jnp.maximum(j - 1, 0)

        @pl.when((j > 0) & (fast_ref[jp] == 1))
        def _():
            for e in range(ne):
                @pl.when(cnt_ref[jp * ne + e] > 0)
                def _(e=e):
                    fast_copy(jp, 1 - slot, e).wait()

    @pl.when(fast_ref[j] == 1)
    def _():
        win_slot = lax.broadcasted_iota(i32, (WIN, tt), 0).astype(f32)
        onehot = jnp.concatenate([(win_slot == targets(e)).astype(bf16) for e in range(ne)], axis=0)
        stage_ref[slot] = jnp.dot(onehot, hn, preferred_element_type=f32).astype(bf16)
        for e in range(ne):
            @pl.when(cnt_ref[j * ne + e] > 0)
            def _(e=e):
                merge_carry(lambda st, sz: stage_ref.at[slot, pl.ds(e * WIN + st, sz)], e)
        wait_previous()
        for e in range(ne):
            @pl.when(cnt_ref[j * ne + e] > 0)
            def _(e=e):
                fast_copy(j, slot, e).start()

        @pl.when(j == nt - 1)
        def _():
            for e in range(ne):
                @pl.when(cnt_ref[j * ne + e] > 0)
                def _(e=e):
                    fast_copy(j, slot, e).wait()

    @pl.when(fast_ref[j] == 0)
    def _():
        wait_previous()
        big_copy = lambda e, rows: pltpu.make_async_copy(
            big_ref.at[e, pl.ds(0, rows)], xe_ref.at[e, pl.ds(_aligned(base_ref[j * ne + e]), rows)], bsem_ref.at[e])
        for e in range(ne):
            for rows, cond in _buckets(cnt_ref[j * ne + e]):
                @pl.when(cond)
                def _(rows=rows, e=e):
                    win_slot = lax.broadcasted_iota(i32, (rows, tt), 0).astype(f32)
                    onehot = (win_slot == targets(e)).astype(bf16)
                    big_ref[e, 0:rows, :] = jnp.dot(onehot, hn, preferred_element_type=f32).astype(bf16)
                    merge_carry(lambda st, sz: big_ref.at[e, pl.ds(st, sz)], e)
                    big_copy(e, rows).start()
        for e in range(ne):
            for rows, cond in _buckets(cnt_ref[j * ne + e]):
                @pl.when(cond)
                def _(rows=rows, e=e):
                    big_copy(e, rows).wait()


def _dispatch(base, cnt, fast, rk, hn, cap):
    t = hn.shape[0]
    nt = t // SEL_TILE
    return pl.pallas_call(
        functools.partial(_dispatch_kernel, cap=cap),
        grid_spec=pltpu.PrefetchScalarGridSpec(
            num_scalar_prefetch=3,
            grid=(nt,),
            in_specs=[pl.BlockSpec((SEL_TILE, LANES), lambda j, b, c, f: (j, 0)),
                      pl.BlockSpec((SEL_TILE, D_MODEL), lambda j, b, c, f: (j, 0))],
            out_specs=pl.BlockSpec(memory_space=pl.ANY),
            scratch_shapes=[pltpu.VMEM((2, N_EXPERTS * WIN, D_MODEL), bf16),
                            pltpu.VMEM((N_EXPERTS, BIG_WIN, D_MODEL), bf16),
                            pltpu.VMEM((N_EXPERTS, ROW_ALIGN, D_MODEL), bf16),
                            pltpu.SemaphoreType.DMA((N_EXPERTS,)),
                            pltpu.SemaphoreType.DMA((N_EXPERTS,))],
        ),
        out_shape=jax.ShapeDtypeStruct((N_EXPERTS, cap + BIG_WIN, D_MODEL), bf16),
        compiler_params=_cparams(("arbitrary",)),
        name="dispatch",
    )(base, cnt, fast, rk, hn)


def _ffn_kernel(x_ref, wg_ref, wu_ref, wd_ref, y_ref):
    x = x_ref[0]
    g = jnp.dot(x, wg_ref[0], preferred_element_type=f32)
    u = jnp.dot(x, wu_ref[0], preferred_element_type=f32)
    h = (g * _sigmoid(g) * u).astype(bf16)
    y_ref[0] = jnp.dot(h, wd_ref[0], preferred_element_type=f32).astype(y_ref.dtype)


def _ffn(xe, wg, wu, wd, cap):
    tr = min(512, cap)
    wspec = lambda a, bb: pl.BlockSpec((1, a, bb), lambda e, i: (e, 0, 0))
    return pl.pallas_call(
        _ffn_kernel,
        grid=(N_EXPERTS, cap // tr),
        in_specs=[pl.BlockSpec((1, tr, D_MODEL), lambda e, i: (e, i, 0)),
                  wspec(D_MODEL, D_EXPERT), wspec(D_MODEL, D_EXPERT), wspec(D_EXPERT, D_MODEL)],
        out_specs=pl.BlockSpec((1, tr, D_MODEL), lambda e, i: (e, i, 0)),
        out_shape=jax.ShapeDtypeStruct((N_EXPERTS, cap, D_MODEL), bf16),
        compiler_params=_cparams(("parallel", "parallel")),
        name="expert_ffn",
    )(xe, wg, wu, wd)


def _combine_kernel(base_ref, cnt_ref, fast_ref, rk_ref, aff_ref, x_ref, gf_ref, ye_ref, out_ref,
                    buf_ref, big_ref, sem_ref, bsem_ref, *, cap, final_norm):
    j = pl.program_id(0)
    nt = pl.num_programs(0)
    tt = SEL_TILE
    ne = N_EXPERTS
    slot = j % 2
    win = min(WIN, cap)
    window_start = lambda base, rows: pl.multiple_of(jnp.minimum(base - base % ROW_ALIGN, cap - rows), ROW_ALIGN)

    def win_copy(jj, sl, e):
        return pltpu.make_async_copy(ye_ref.at[e, pl.ds(window_start(base_ref[jj * ne + e], win), win)],
                                     buf_ref.at[sl, pl.ds(e * win, win)], sem_ref.at[sl])

    @pl.when((j == 0) & (fast_ref[0] == 1))
    def _():
        for e in range(ne):
            win_copy(0, 0, e).start()

    jn = jnp.minimum(j + 1, nt - 1)

    @pl.when((j + 1 < nt) & (fast_ref[jn] == 1))
    def _():
        for e in range(ne):
            win_copy(jn, 1 - slot, e).start()

    @pl.when(fast_ref[j] == 1)
    def _():
        rkt = rk_ref[...].T
        afft = aff_ref[...].T
        win_slot = lax.broadcasted_iota(i32, (win, tt), 0).astype(f32)
        his, los = [], []
        for e in range(ne):
            base = base_ref[j * ne + e]
            shift = (base - window_start(base, win)).astype(f32)
            row = rkt[e:e + 1, :]
            gate = jnp.where((win_slot == row + shift) & (row >= 0.0), afft[e:e + 1, :], 0.0)
            hi = gate.astype(bf16)
            his.append(hi)
            los.append((gate - hi.astype(f32)).astype(bf16))
        lhs = jnp.concatenate([jnp.concatenate(his, axis=0), jnp.concatenate(los, axis=0)], axis=1)
        for e in range(ne):
            win_copy(j, slot, e).wait()
        y = lax.dot_general(lhs, buf_ref[slot], (((0,), (0,)), ((), ())), preferred_element_type=f32)
        out_ref[...] = x_ref[...] + y[0:tt] + y[tt:2 * tt]

    @pl.when(fast_ref[j] == 0)
    def _():
        buckets = lambda n: tuple((min(rows, cap), cond) for rows, cond in _buckets(n))
        big_copy = lambda e, rows: pltpu.make_async_copy(
            ye_ref.at[e, pl.ds(window_start(base_ref[j * ne + e], rows), rows)], big_ref.at[e, pl.ds(0, rows)],
            bsem_ref.at[e])
        for e in range(ne):
            for rows, cond in buckets(cnt_ref[j * ne + e]):
                @pl.when(cond)
                def _(rows=rows, e=e):
                    big_copy(e, rows).start()
        out_ref[...] = x_ref[...]
        for e in range(ne):
            for rows, cond in buckets(cnt_ref[j * ne + e]):
                @pl.when(cond)
                def _(rows=rows, e=e):
                    base = base_ref[j * ne + e]
                    big_copy(e, rows).wait()
                    col = rk_ref[:, e:e + 1]
                    tgt = col + (base - window_start(base, rows)).astype(f32)
                    win_slot = lax.broadcasted_iota(i32, (tt, rows), 1).astype(f32)
                    onehot = ((tgt == win_slot) & (col >= 0.0)).astype(bf16)
                    contrib = jnp.dot(onehot, big_ref[e, 0:rows, :], preferred_element_type=f32)
                    out_ref[...] += contrib * aff_ref[:, e:e + 1]

    if final_norm:
        y = out_ref[...]
        ms = jnp.mean(y * y, axis=-1, keepdims=True)
        out_ref[...] = y * lax.rsqrt(ms + EPS) * gf_ref[...]


def _combine(base, cnt, fast, rk, aff, x2, gfin, ye, cap, final_norm):
    t = x2.shape[0]
    nt = t // SEL_TILE
    tile = lambda width: pl.BlockSpec((SEL_TILE, width), lambda j, b, c, f: (j, 0))
    return pl.pallas_call(
        functools.partial(_combine_kernel, cap=cap, final_norm=final_norm),
        grid_spec=pltpu.PrefetchScalarGridSpec(
            num_scalar_prefetch=3,
            grid=(nt,),
            in_specs=[tile(LANES), tile(LANES), tile(D_MODEL),
                      pl.BlockSpec((1, D_MODEL), lambda j, b, c, f: (0, 0)),
                      pl.BlockSpec(memory_space=pl.ANY)],
            out_specs=tile(D_MODEL),
            scratch_shapes=[pltpu.VMEM((2, N_EXPERTS * min(WIN, cap), D_MODEL), bf16),
                            pltpu.VMEM((N_EXPERTS, min(BIG_WIN, cap), D_MODEL), bf16),
                            pltpu.SemaphoreType.DMA((2,)),
                            pltpu.SemaphoreType.DMA((N_EXPERTS,))],
        ),
        out_shape=jax.ShapeDtypeStruct((t, D_MODEL), f32),
        compiler_params=_cparams(("arbitrary",)),
        name="combine",
    )(base, cnt, fast, rk, aff, x2, gfin, ye)


def _split_bf16(w):
    hi = w.astype(bf16)
    return jnp.concatenate([hi, (w - hi.astype(f32)).astype(bf16)], axis=1)


def _prep_layer(l, norm_mix, w_in, conv_w, a_log, dt_bias, a_norm, w_proj_a, w_proj_b, w_out,
                norm_ffn, w_router, w_gate, w_up, w_down):
    w = w_in[l]
    small0 = 4 * A_WIDTH
    small1 = small0 + 4 * A_HEADS
    qkvb1 = small1 + 3 * B_HEADS * HEAD_DIM
    qb1 = small1 + B_HEADS * HEAD_DIM
    w_main = jnp.concatenate([w[:, :small0], w[:, qkvb1:], w[:, small1:qb1] * (HEAD_DIM ** -0.5), w[:, qb1:qkvb1]],
                             axis=1).astype(bf16)
    w_small = jnp.pad(w[:, small0:small1], ((0, 0), (0, LANES - 4 * A_HEADS))).astype(bf16)
    par = jnp.zeros((8, LANES), f32)
    par = par.at[0, 2 * A_HEADS:4 * A_HEADS].set(jnp.exp(a_log[l].astype(f32)).reshape(-1))
    par = par.at[1, 2 * A_HEADS:4 * A_HEADS].set(dt_bias[l].astype(f32).reshape(-1))
    return dict(
        g_mix=norm_mix[l].reshape(1, D_MODEL), w_main=w_main, w_small=w_small, conv_w=conv_w[l], par=par,
        an=jnp.tile(a_norm[l], 2).reshape(1, LANES),
        wpa=w_proj_a[l].astype(bf16), wpb=w_proj_b[l].astype(bf16), wout=w_out[l].astype(bf16),
        g_ffn=norm_ffn[l].reshape(1, D_MODEL),
        wr=_split_bf16(jnp.pad(w_router[l], ((0, 0), (0, LANES - N_EXPERTS)))),
        wg=w_gate[l].astype(bf16), wu=w_up[l].astype(bf16), wd=w_down[l].astype(bf16),
    )


def _layer(x2, b, s, lw, gfin, final_norm):
    t = b * s
    main2, small2 = _inproj(x2, lw["g_mix"], lw["w_main"], lw["w_small"])
    main3 = main2.reshape(b, s, MAIN_WIDTH)
    qkvn, gb3, phase_views = _prep(main3, small2.reshape(b, s, LANES), lw["conv_w"], lw["par"])
    oa = _delta(qkvn, main3, gb3, lw["an"])
    obs, lses = [], []
    for g in range(len(B_GROUPS)):
        o_g, l_g = _attention(main3, g, phase_views.get(g))
        obs.append(o_g)
        lses.append(l_g)
    x2, hn, aff, afft = _merge_router(x2, oa.reshape(t, A_WIDTH), obs, lses, main2, lw["wpa"], lw["wpb"], lw["wout"],
                                      lw["g_ffn"], lw["wr"])
    cap = (CAPACITY_FACTOR * t) // N_EXPERTS
    thr, need = _threshold(afft, cap)
    pad_row = lambda a: jnp.pad(a[:, 0], (0, LANES - N_EXPERTS)).reshape(1, LANES)
    rk, tb = _select(aff, pad_row(thr), pad_row(need))
    base = tb[:, 0, :N_EXPERTS].reshape(-1)
    cnt = tb[:, 1, :N_EXPERTS].reshape(-1)
    fast = (jnp.max(tb[:, 1, :N_EXPERTS], axis=1) <= SMALL_BUCKET).astype(i32)
    xe = _dispatch(base, cnt, fast, rk, hn, cap)
    ye = _ffn(xe, lw["wg"], lw["wu"], lw["wd"], cap)
    return _combine(base, cnt, fast, rk, aff, x2, gfin, ye, cap, final_norm)


def kernel(x_prompt, x_sample, norm_mix, w_in, conv_w, a_log, dt_bias, a_norm, w_proj_a, w_proj_b, w_out,
           norm_ffn, w_router, w_gate, w_up, w_down, norm_final):
    depth = w_in.shape[0]
    layers = [_prep_layer(l, norm_mix, w_in, conv_w, a_log, dt_bias, a_norm, w_proj_a, w_proj_b, w_out,
                          norm_ffn, w_router, w_gate, w_up, w_down) for l in range(depth)]
    gfin = norm_final.reshape(1, D_MODEL)
    outs = []
    for x in (x_prompt, x_sample):
        b, s, d = x.shape
        x2 = x.reshape(b * s, d)
        for l in range(depth):
            x2 = _layer(x2, b, s, layers[l], gfin, l == depth - 1)
        outs.append(x2.reshape(b, s, d))
    return tuple(outs)
```

```python
import functools

import numpy as np
import jax
import jax.numpy as jnp
from jax import lax
from jax.experimental import pallas as pl
from jax.experimental.pallas import tpu as pltpu

f32 = jnp.float32
bf16 = jnp.bfloat16
i32 = jnp.int32
HIGHEST = lax.Precision.HIGHEST

D_MODEL = 1024
A_HEADS = 8
HEAD_DIM = 64
A_WIDTH = A_HEADS * HEAD_DIM
A_CONV = 5
CHUNK = 64
B_GROUPS = ((128, 1), (512, 4), (2048, 16))
B_HEADS_PER_GROUP = 4
B_HEADS = B_HEADS_PER_GROUP * len(B_GROUPS)
B_GROUP_WIDTH = B_HEADS_PER_GROUP * HEAD_DIM
B_SIDE = 64
N_EXPERTS = 16
D_EXPERT = 1024
CAPACITY_FACTOR = 2
EPS = 1e-6
NEG = -1e30

LANES = 128
MAIN_WIDTH = 2 * D_MODEL + 4 * A_WIDTH + 3 * B_HEADS * HEAD_DIM
Z_BLK = 3 * A_WIDTH // LANES
GATE_BLK = 4 * A_WIDTH // D_MODEL
QKV_B_BLK256 = (2 * D_MODEL + 4 * A_WIDTH) // B_GROUP_WIDTH
MAIN_BLK256 = MAIN_WIDTH // B_GROUP_WIDTH
N_TILE = 1280
SEL_TILE = 256
SELECT_SUB_TILES = 4
SMALL_BUCKET = 64
ROW_ALIGN = 16
VMEM_LIMIT = 56 * 1024 * 1024


def _cparams(sem):
    return pltpu.CompilerParams(dimension_semantics=sem, vmem_limit_bytes=VMEM_LIMIT)


def _sigmoid(x):
    return 1.0 / (1.0 + jnp.exp(-x))


def _softplus(x):
    return jnp.maximum(x, 0.0) + jnp.log(1.0 + jnp.exp(-jnp.abs(x)))


def _inproj_kernel(x_ref, g_ref, w_ref, ws_ref, o_ref, os_ref, n_ref):
    @pl.when(pl.program_id(1) == 0)
    def _():
        x = x_ref[...]
        ms = jnp.mean(x * x, axis=-1, keepdims=True)
        n = (x * lax.rsqrt(ms + EPS) * g_ref[...]).astype(bf16)
        n_ref[...] = n
        os_ref[...] = jnp.dot(n, ws_ref[...], preferred_element_type=f32)

    o_ref[...] = jnp.dot(n_ref[...], w_ref[...], preferred_element_type=f32).astype(o_ref.dtype)


def _inproj(x2, g, w_main, w_small):
    t = x2.shape[0]
    tm = min(1024, t)
    return pl.pallas_call(
        _inproj_kernel,
        grid=(t // tm, MAIN_WIDTH // N_TILE),
        in_specs=[
            pl.BlockSpec((tm, D_MODEL), lambda i, j: (i, 0)),
            pl.BlockSpec((1, D_MODEL), lambda i, j: (0, 0)),
            pl.BlockSpec((D_MODEL, N_TILE), lambda i, j: (0, j)),
            pl.BlockSpec((D_MODEL, LANES), lambda i, j: (0, 0)),
        ],
        out_specs=[
            pl.BlockSpec((tm, N_TILE), lambda i, j: (i, j)),
            pl.BlockSpec((tm, LANES), lambda i, j: (i, 0)),
        ],
        out_shape=[jax.ShapeDtypeStruct((t, MAIN_WIDTH), bf16), jax.ShapeDtypeStruct((t, LANES), f32)],
        scratch_shapes=[pltpu.VMEM((tm, D_MODEL), bf16)],
        compiler_params=_cparams(("parallel", "arbitrary")),
        name="inproj",
    )(x2, g, w_main, w_small)


HALO = 16
PREP_ROWS = 256
BETA_LANE = 0
CUM_LANE = 4 * A_HEADS


def _prep_kernel(cur_ref, prev_ref, next_ref, w_ref, sm_ref, par_ref, *rest, ts, n_tiles):
    dilated = [(g, dil) for g, (_, dil) in enumerate(B_GROUPS) if dil > 1]
    qkv_refs = rest[:3 * len(dilated)]
    o_ref, gb_ref = rest[3 * len(dilated):3 * len(dilated) + 2]
    ph_refs = rest[3 * len(dilated) + 2:3 * len(dilated) + 2 + len(dilated)]
    a_ref, s_ref = rest[-2:]
    i = pl.program_id(1)
    first = (i > 0).astype(f32)
    last = (i < n_tiles - 1).astype(f32)
    pad = (A_CONV - 1) // 2
    r = lax.broadcasted_iota(i32, (LANES, LANES), 0) // HEAD_DIM
    cc = lax.broadcasted_iota(i32, (LANES, LANES), 1) // HEAD_DIM
    head_ones = (r == cc).astype(f32)
    sub = min(PREP_ROWS, ts)
    for c in range(3 * A_WIDTH // LANES):
        cols = slice(c * LANES, (c + 1) * LANES)
        a_ref[0:HALO, :] = prev_ref[0, :, cols].astype(f32) * first
        a_ref[HALO:HALO + ts, :] = cur_ref[0, :, cols].astype(f32)
        a_ref[HALO + ts:2 * HALO + ts, :] = next_ref[0, :, cols].astype(f32) * last
        w = w_ref[:, cols]

        for k in range(ts // sub):
            r0 = k * sub
            y = jnp.zeros((sub, LANES), f32)
            for j in range(A_CONV):
                off = r0 + HALO - pad + j
                y = y + a_ref[off:off + sub, :] * w[j:j + 1]
            y = y * _sigmoid(y)
            if c < 2 * A_WIDTH // LANES:
                ss = jnp.dot(y * y, head_ones, preferred_element_type=f32, precision=HIGHEST)
                qscale = HEAD_DIM ** -0.5 if c < A_WIDTH // LANES else 1.0
                y = y * (lax.rsqrt(ss + EPS) * qscale)
            o_ref[0, r0:r0 + sub, cols] = y.astype(o_ref.dtype)

    ch = CHUNK
    ri = lax.broadcasted_iota(i32, (ch, ch), 0)
    ci = lax.broadcasted_iota(i32, (ch, ch), 1)
    lower = (ri >= ci).astype(f32)
    upper = (ri <= ci).astype(f32)
    lane = lax.broadcasted_iota(i32, (ch, LANES), 1)
    g_lane = 2 * A_HEADS
    is_fwd = lane < g_lane + A_HEADS
    for k in range(ts // ch):
        rows = slice(k * ch, (k + 1) * ch)
        sm = sm_ref[0, rows, :]
        g = -par_ref[0:1, :] * _softplus(sm + par_ref[1:2, :])
        cum = jnp.where(is_fwd,
                        jnp.dot(lower, g, preferred_element_type=f32, precision=HIGHEST),
                        jnp.dot(upper, g, preferred_element_type=f32, precision=HIGHEST))
        cum = pltpu.roll(cum, CUM_LANE - g_lane, axis=1)
        gb_ref[0, rows, :] = jnp.where(lane < g_lane, _sigmoid(sm), cum)

    w = B_GROUP_WIDTH
    for gi, (g, dil) in enumerate(dilated):
        for off in range(3):
            x = qkv_refs[3 * gi + off]
            for h in range(w // LANES):
                s_ref[h] = x[0, :, h * LANES:(h + 1) * LANES].astype(f32)
            for r in range(dil):
                for h in range(w // LANES):
                    c0 = (r * 3 + off) * w + h * LANES
                    ph_refs[gi][0, :, c0:c0 + LANES] = s_ref[h, pl.ds(r, ts // dil, stride=dil), :].astype(bf16)


def _prep(main3, small3, conv_w, par):
    b, s, _ = main3.shape
    ts = min(1024, s)
    n_tiles = s // ts
    hb = ts // HALO
    wa = 3 * A_WIDTH
    w = B_GROUP_WIDTH
    ngrp = len(B_GROUPS)
    dilated = [(g, dil) for g, (_, dil) in enumerate(B_GROUPS) if dil > 1]
    qkv_specs = [pl.BlockSpec((1, ts, w), functools.partial(lambda bi, i, blk: (bi, i, blk), blk=QKV_B_BLK256 + off * ngrp + g))
                 for g, _ in dilated for off in range(3)]
    outs = pl.pallas_call(
        functools.partial(_prep_kernel, ts=ts, n_tiles=n_tiles),
        grid=(b, n_tiles),
        in_specs=[
            pl.BlockSpec((1, ts, wa), lambda bi, i: (bi, i, 0)),
            pl.BlockSpec((1, HALO, wa), lambda bi, i: (bi, jnp.maximum(i * hb - 1, 0), 0)),
            pl.BlockSpec((1, HALO, wa), lambda bi, i: (bi, jnp.minimum((i + 1) * hb, s // HALO - 1), 0)),
            pl.BlockSpec((A_CONV, wa), lambda bi, i: (0, 0)),
            pl.BlockSpec((1, ts, LANES), lambda bi, i: (bi, i, 0)),
            pl.BlockSpec((8, LANES), lambda bi, i: (0, 0)),
        ] + qkv_specs,
        out_specs=[pl.BlockSpec((1, ts, wa), lambda bi, i: (bi, i, 0)),
                   pl.BlockSpec((1, ts, LANES), lambda bi, i: (bi, i, 0))]
        + [pl.BlockSpec((1, ts // dil, dil * 3 * w), lambda bi, i: (bi, i, 0)) for _, dil in dilated],
        out_shape=[jax.ShapeDtypeStruct((b, s, wa), bf16), jax.ShapeDtypeStruct((b, s, LANES), f32)]
        + [jax.ShapeDtypeStruct((b, s // dil, dil * 3 * w), bf16) for _, dil in dilated],
        scratch_shapes=[pltpu.VMEM((ts + 2 * HALO, LANES), f32), pltpu.VMEM((w // LANES, ts, LANES), f32)],
        compiler_params=_cparams(("parallel", "parallel")),
        name="prep",
    )(main3, main3, main3, conv_w, small3, par, *([main3] * len(qkv_specs)))
    return outs[0], outs[1], {g: ph for (g, _), ph in zip(dilated, outs[2:])}


GROUP = 8


def _split_heads(x, mask):
    z = jnp.zeros_like(x)
    return jnp.concatenate([jnp.where(mask, x, z), jnp.where(mask, z, x)], axis=0)


def _delta_stages(pair):
    c = CHUNK
    lane = lax.broadcasted_iota(i32, (c, LANES), 1)
    m0 = lane < HEAD_DIM
    m0w = jnp.concatenate([m0, m0], axis=1)
    rl = lax.broadcasted_iota(i32, (c, LANES), 0)
    cl = lane % HEAD_DIM
    eye_p = (rl == cl).astype(f32)
    same_head = (lax.broadcasted_iota(i32, (LANES, LANES), 0) // HEAD_DIM
                 == lax.broadcasted_iota(i32, (LANES, LANES), 1) // HEAD_DIM)
    bd = lambda x: _split_heads(x, m0).astype(bf16)
    bdw = lambda x: _split_heads(x, m0w).astype(bf16)

    def pick(full, base):
        c0 = jnp.sum(jnp.where(lane == base + 2 * pair, full, 0.0), axis=1, keepdims=True)
        c1 = jnp.sum(jnp.where(lane == base + 2 * pair + 1, full, 0.0), axis=1, keepdims=True)
        return jnp.where(m0, c0, c1)

    def gates(st):
        d = 1 if st["upper"] else 0
        st["beta"] = pick(st["gb"], BETA_LANE + d * A_HEADS)
        st["cum"] = pick(st["gb"], CUM_LANE + d * A_HEADS)

    def gram(st):
        q, k = st["q"], st["k"]
        st["gq"] = lax.dot_general(jnp.concatenate([q, k], axis=0), _split_heads(k, m0), (((1,), (1,)), ((), ())),
                                   preferred_element_type=f32)

    def decay(st):
        cum = st["cum"]
        cum_t = cum.T
        cum_row = jnp.concatenate([cum_t[0:1], cum_t[HEAD_DIM:HEAD_DIM + 1]], axis=1)
        incl = (rl <= cl) if st["upper"] else (rl >= cl)
        strict = (rl < cl) if st["upper"] else (rl > cl)
        dec = jnp.where(incl, jnp.exp(jnp.minimum(cum - cum_row, 0.0)), 0.0)
        gq = st.pop("gq")
        st["attn"] = (gq[0:c] * dec).astype(bf16)
        a = jnp.where(strict, -(st["beta"] * gq[c:2 * c] * dec), 0.0)
        st["p"] = eye_p + a
        st["a"] = jnp.dot(a.astype(bf16), bd(a), preferred_element_type=f32)

    def double(st):
        a, p = st["a"], st["p"]
        y = jnp.dot(jnp.concatenate([a, p], axis=0).astype(bf16), bd(a), preferred_element_type=f32)
        st["a"] = y[0:c]
        st["p"] = p + y[c:2 * c]

    def solve(st):
        a, p = st.pop("a"), st.pop("p")
        tinv = (p + jnp.dot(p.astype(bf16), bd(a), preferred_element_type=f32)).astype(bf16)
        cum = st["cum"]
        st["gtot"] = cum[0:1] if st["upper"] else cum[c - 1:c]
        st["eg"] = jnp.exp(cum)
        kf = st["k"].astype(f32)
        rhs = jnp.concatenate([st["v"].astype(f32) * st["beta"], kf * st["beta"] * st["eg"]], axis=1)
        st["uw"] = jnp.dot(tinv, bdw(rhs), preferred_element_type=f32)

    def finish(st):
        uw = st.pop("uw")
        aw = jnp.dot(st.pop("attn"), bdw(uw), preferred_element_type=f32)
        kd = (st["k"].astype(f32) * jnp.exp(st["gtot"] - st["cum"])).astype(bf16)
        t = lax.dot_general(kd, uw.astype(bf16), (((0,), (0,)), ((), ())), preferred_element_type=f32)
        st["oin"] = aw[:, 0:LANES]
        st["qeff"] = (st["q"].astype(f32) * st["eg"] - aw[:, LANES:2 * LANES]).astype(bf16)
        st["bbd"] = jnp.where(same_head, t[:, 0:LANES], 0.0)
        st["abd"] = jnp.where(same_head, -t[:, LANES:2 * LANES], 0.0).astype(bf16)
        st["dec"] = jnp.exp(st["gtot"])

    n_double = int(np.log2(c)) - 2
    return [gates, gram, decay] + [double] * n_double + [solve, finish]


def _delta_kernel(q_ref, k_ref, v_ref, z_ref, gb_ref, an_ref, o_ref,
                  acc_ref, st_ref, qe_ref, oi_ref, ab_ref, bb_ref, dc_ref, *, s):
    c = CHUNK
    n = s // c
    g = min(GROUP, n)
    ng = n // g
    pair = pl.program_id(1)
    acc_ref[...] = jnp.zeros_like(acc_ref)
    st_ref[...] = jnp.zeros_like(st_ref)
    stages = _delta_stages(pair)

    def row_start(gi, t, d):
        cidx = gi * g + t if d == 0 else n - 1 - (gi * g + t)
        return pl.multiple_of(cidx * c, c)

    def group_step(gi_a, slot_a, gi_b, slot_b):
        streams = []
        if gi_a is not None:
            for t in range(g):
                for d in range(2):
                    r0 = row_start(gi_a, t, d)
                    streams.append(dict(
                        q=q_ref[0, pl.ds(r0, c), :], k=k_ref[0, pl.ds(r0, c), :], v=v_ref[0, pl.ds(r0, c), :],
                        gb=gb_ref[0, pl.ds(r0, c), :], upper=(d == 1), idx=(d * 2 + slot_a) * g + t))
        states = [st_ref[0], st_ref[1]] if gi_b is not None else None

        def recurrence(t):
            for d in range(2):
                r0 = row_start(gi_b, t, d)
                idx = (d * 2 + slot_b) * g + t
                y = jnp.dot(jnp.concatenate([ab_ref[idx], qe_ref[idx]], axis=0), states[d].astype(bf16),
                            preferred_element_type=f32)
                acc_ref[pl.ds(r0, c), :] += y[LANES:LANES + c] + oi_ref[idx]
                states[d] = states[d] * dc_ref[idx][0:1] + y[0:LANES] + bb_ref[idx]

        done_b = 0
        for si, stage in enumerate(stages):
            for st in streams:
                stage(st)
            if gi_b is not None and si >= 1 and done_b < g:
                recurrence(done_b)
                done_b += 1
        if gi_b is not None:
            for t in range(done_b, g):
                recurrence(t)
            st_ref[0] = states[0]
            st_ref[1] = states[1]
        for st in streams:
            idx = st["idx"]
            qe_ref[idx] = st["qeff"]
            oi_ref[idx] = st["oin"]
            ab_ref[idx] = st["abd"]
            bb_ref[idx] = st["bbd"]
            dc_ref[idx] = jnp.broadcast_to(st["dec"], (8, LANES))

    group_step(0, 0, None, None)

    def body(i, carry):
        group_step(i, i % 2, i - 1, (i - 1) % 2)
        return carry

    lax.fori_loop(1, ng, body, 0)
    group_step(None, None, ng - 1, (ng - 1) % 2)

    rows = min(256, s)
    r = lax.broadcasted_iota(i32, (LANES, LANES), 0) // HEAD_DIM
    cc = lax.broadcasted_iota(i32, (LANES, LANES), 1) // HEAD_DIM
    head_mean = (r == cc).astype(f32) * (1.0 / HEAD_DIM)

    def epi(i, carry):
        r0 = pl.multiple_of(i * rows, rows)
        o = acc_ref[pl.ds(r0, rows), :]
        ms = jnp.dot(o * o, head_mean, preferred_element_type=f32, precision=HIGHEST)
        z = z_ref[0, pl.ds(r0, rows), :].astype(f32)
        y = o * lax.rsqrt(ms + EPS) * an_ref[...] * (z * _sigmoid(z))
        o_ref[0, pl.ds(r0, rows), :] = y.astype(o_ref.dtype)
        return carry

    lax.fori_loop(0, s // rows, epi, 0)


def _delta(qkvn, main3, gb3, an):
    b, s, _ = qkvn.shape
    npair = A_HEADS // 2
    kb = A_WIDTH // LANES
    nbuf = 4 * min(GROUP, s // CHUNK)
    seq = lambda off: pl.BlockSpec((1, s, LANES), lambda bi, p: (bi, 0, off + p))
    return pl.pallas_call(
        functools.partial(_delta_kernel, s=s),
        grid=(b, npair),
        in_specs=[
            seq(0), seq(kb), seq(2 * kb), seq(Z_BLK),
            pl.BlockSpec((1, s, LANES), lambda bi, p: (bi, 0, 0)),
            pl.BlockSpec((1, LANES), lambda bi, p: (0, 0)),
        ],
        out_specs=pl.BlockSpec((1, s, LANES), lambda bi, p: (bi, 0, p)),
        out_shape=jax.ShapeDtypeStruct((b, s, A_WIDTH), bf16),
        scratch_shapes=[pltpu.VMEM((s, LANES), f32), pltpu.VMEM((2, LANES, LANES), f32),
                        pltpu.VMEM((nbuf, CHUNK, LANES), bf16), pltpu.VMEM((nbuf, CHUNK, LANES), f32),
                        pltpu.VMEM((nbuf, LANES, LANES), bf16), pltpu.VMEM((nbuf, LANES, LANES), f32),
                        pltpu.VMEM((nbuf, 8, LANES), f32)],
        compiler_params=_cparams(("parallel", "arbitrary")),
        name="delta",
    )(qkvn, qkvn, qkvn, main3, gb3, an)


def _attn_kernel(*refs, lp, bq, nk, dil, slopes, nph, fused):
    w = B_GROUP_WIDTH
    nh = B_HEADS_PER_GROUP
    if fused:
        x_ref, o_ref, l_ref, bias_ref = refs
        src = lambda j, off: (x_ref, (j * 3 + off) * w)
    else:
        q_ref, k_ref, v_ref, o_ref, l_ref, bias_ref = refs
        src = lambda j, off: ((q_ref, k_ref, v_ref)[off], 0)
    head_of_lane = lax.broadcasted_iota(i32, (1, w), 1) // HEAD_DIM
    nq = lp // bq
    row = lax.broadcasted_iota(i32, (bq, nk), 0)
    col = lax.broadcasted_iota(i32, (bq, nk), 1)

    def alibi_bias(offset):
        adelta = jnp.abs(col - row + offset)
        dist = adelta.astype(f32) * float(dil)
        return [jnp.where(adelta <= B_SIDE, -slopes[h] * dist, NEG) for h in range(nh)]

    interior = -B_SIDE
    has_interior = nq > 2 and nk == bq + 2 * B_SIDE
    if has_interior:
        for h, b in enumerate(alibi_bias(interior)):
            bias_ref[h] = b

    def load(j, off, start, size):
        ref, lane0 = src(j, off)
        return ref[0, pl.ds(start, size), lane0:lane0 + w]

    def blocks(items, hoisted):
        work = []
        for j, i in items:
            q0 = pl.multiple_of(i * bq, bq)
            ks = pl.multiple_of(jnp.clip(q0 - B_SIDE, 0, lp - nk), B_SIDE)
            q = load(j, 0, q0, bq)
            zq = jnp.zeros_like(q)
            qs = jnp.concatenate([jnp.where(head_of_lane == h, q, zq) for h in range(nh)], axis=0)
            sc = lax.dot_general(qs, load(j, 1, ks, nk), (((1,), (1,)), ((), ())), preferred_element_type=f32)
            bias = [bias_ref[h] for h in range(nh)] if hoisted else alibi_bias(ks - q0)
            work.append(dict(j=j, q0=q0, ks=ks, sc=sc, bias=bias,
                             o=jnp.zeros((bq, w), f32), lse=jnp.zeros((bq, w), f32)))
        for h in range(nh):
            hm = head_of_lane == h
            for wk in work:
                sh = wk["sc"][h * bq:(h + 1) * bq] + wk["bias"][h]
                m = jnp.max(sh, axis=1, keepdims=True)
                p = jnp.exp(sh - m)
                l = jnp.sum(p, axis=1, keepdims=True)
                oh = jnp.dot(p.astype(bf16), load(wk["j"], 2, wk["ks"], nk), preferred_element_type=f32) / l
                wk["o"] = jnp.where(hm, oh, wk["o"])
                wk["lse"] = jnp.where(hm, m + jnp.log(l), wk["lse"])
        for wk in work:
            lanes = slice(wk["j"] * w, (wk["j"] + 1) * w)
            o_ref[0, pl.ds(wk["q0"], bq), lanes] = wk["o"].astype(o_ref.dtype)
            l_ref[0, pl.ds(wk["q0"], bq), lanes] = wk["lse"]

    if nq == 1:
        for j in range(0, nph, 2):
            blocks([(jj, 0) for jj in range(j, min(j + 2, nph))], False)
        return
    assert nq % 2 == 0
    npair = nq // 2
    for j in range(nph):
        def body(pi, carry, j=j):
            items = [(j, 2 * pi), (j, 2 * pi + 1)]
            if has_interior:
                is_interior = (pi > 0) & (pi < npair - 1)

                @pl.when(is_interior)
                def _():
                    blocks(items, True)

                @pl.when(jnp.logical_not(is_interior))
                def _():
                    blocks(items, False)
            else:
                blocks(items, False)
            return carry

        lax.fori_loop(0, npair, body, 0)


ATTN_ROWS = 4096


def _attention(main3, group, phase_view):
    b, s, _ = main3.shape
    _, dil = B_GROUPS[group]
    lp = s // dil
    bq = min(128, lp)
    nk = min(bq + 2 * B_SIDE, lp)
    slopes = tuple(float(2.0 ** (-8.0 * (group * B_HEADS_PER_GROUP + h + 1) / B_HEADS))
                   for h in range(B_HEADS_PER_GROUP))
    w = B_GROUP_WIDTH
    ngrp = len(B_GROUPS)
    fused = dil > 1
    nph = max(1, min(dil, ATTN_ROWS // lp)) if fused else 1
    if fused:
        srcs = [phase_view]
        in_specs = [pl.BlockSpec((1, lp, nph * 3 * w), lambda bi, r: (bi, 0, r))]
    else:
        srcs = [main3] * 3
        in_specs = [pl.BlockSpec((1, lp, w), functools.partial(lambda bi, r, blk: (bi, 0, blk),
                                                                blk=QKV_B_BLK256 + off * ngrp + group))
                    for off in range(3)]
    o, lse = pl.pallas_call(
        functools.partial(_attn_kernel, lp=lp, bq=bq, nk=nk, dil=dil, slopes=slopes, nph=nph, fused=fused),
        grid=(b, dil // nph),
        in_specs=in_specs,
        out_specs=[pl.BlockSpec((1, lp, nph * w), lambda bi, r: (bi, 0, r)),
                   pl.BlockSpec((1, lp, nph * w), lambda bi, r: (bi, 0, r))],
        out_shape=[jax.ShapeDtypeStruct((b, lp, dil * w), bf16), jax.ShapeDtypeStruct((b, lp, dil * w), f32)],
        scratch_shapes=[pltpu.VMEM((B_HEADS_PER_GROUP, bq, nk), f32)],
        compiler_params=_cparams(("parallel", "parallel")),
        name=f"attn_d{dil}",
    )(*srcs)
    return o.reshape(b * s, w), lse.reshape(b * s, w)


def _merge_kernel(x_ref, oa_ref, o1_ref, o2_ref, o3_ref, l1_ref, l2_ref, l3_ref, ga_ref, gb_ref,
                  wpa_ref, wpb_ref, wout_ref, g_ref, wr_ref, out_ref, hn_ref, aff_ref, afft_ref):
    l1, l2, l3 = l1_ref[...], l2_ref[...], l3_ref[...]
    m = jnp.maximum(jnp.maximum(l1, l2), l3)
    e1, e2, e3 = jnp.exp(l1 - m), jnp.exp(l2 - m), jnp.exp(l3 - m)
    ob = (e1 * o1_ref[...].astype(f32) + e2 * o2_ref[...].astype(f32) + e3 * o3_ref[...].astype(f32)) / (e1 + e2 + e3)
    ya = jnp.dot(oa_ref[...], wpa_ref[...], preferred_element_type=f32)
    yb = jnp.dot(ob.astype(bf16), wpb_ref[...], preferred_element_type=f32)
    mix = _sigmoid(ga_ref[...].astype(f32)) * ya + _sigmoid(gb_ref[...].astype(f32)) * yb
    x = x_ref[...] + jnp.dot(mix.astype(bf16), wout_ref[...], preferred_element_type=f32)
    out_ref[...] = x

    ms = jnp.mean(x * x, axis=-1, keepdims=True)
    hn = x * lax.rsqrt(ms + EPS) * g_ref[...]
    hn_ref[...] = hn.astype(bf16)
    hi = hn.astype(bf16)
    lo = (hn - hi.astype(f32)).astype(bf16)
    tm = hn.shape[0]
    prod = jnp.dot(jnp.concatenate([hi, lo], axis=0), wr_ref[...], preferred_element_type=f32)
    logits = (prod[0:tm, 0:LANES] + prod[0:tm, LANES:2 * LANES]) + prod[tm:2 * tm, 0:LANES]
    lane = lax.broadcasted_iota(i32, logits.shape, 1)
    logits = jnp.where(lane < N_EXPERTS, logits, NEG)
    mx = jnp.max(logits, axis=1, keepdims=True)
    e = jnp.exp(logits - mx)
    aff = e / jnp.sum(e, axis=1, keepdims=True)
    aff_ref[...] = aff
    afft_ref[...] = aff.T[0:N_EXPERTS]


def _merge_router(x2, oa2, obs, lses, main2, wpa, wpb, wout, g_ffn, wr):
    t = x2.shape[0]
    tm = min(512, t)
    w = B_GROUP_WIDTH
    row = lambda width: pl.BlockSpec((tm, width), lambda i: (i, 0))
    full = lambda a, bb: pl.BlockSpec((a, bb), lambda i: (0, 0))
    return pl.pallas_call(
        _merge_kernel,
        grid=(t // tm,),
        in_specs=[row(D_MODEL), row(A_WIDTH), row(w), row(w), row(w), row(w), row(w), row(w),
                  pl.BlockSpec((tm, D_MODEL), lambda i: (i, GATE_BLK)),
                  pl.BlockSpec((tm, D_MODEL), lambda i: (i, GATE_BLK + 1)),
                  full(A_WIDTH, D_MODEL), full(w, D_MODEL), full(D_MODEL, D_MODEL),
                  full(1, D_MODEL), full(D_MODEL, 2 * LANES)],
        out_specs=[row(D_MODEL), row(D_MODEL), row(LANES), pl.BlockSpec((N_EXPERTS, tm), lambda i: (0, i))],
        out_shape=[jax.ShapeDtypeStruct((t, D_MODEL), f32), jax.ShapeDtypeStruct((t, D_MODEL), bf16),
                   jax.ShapeDtypeStruct((t, LANES), f32), jax.ShapeDtypeStruct((N_EXPERTS, t), f32)],
        compiler_params=_cparams(("parallel",)),
        name="merge_router",
    )(x2, oa2, *obs, *lses, main2, main2, wpa, wpb, wout, g_ffn, wr)


def _threshold_kernel(afft_ref, thr_ref, need_ref, *, cap):
    bits = lax.bitcast_convert_type(afft_ref[...], i32)

    def step(i, lo):
        cand = lo | lax.shift_left(jnp.int32(1), 30 - i)
        cnt = jnp.sum((bits >= cand).astype(i32), axis=1, keepdims=True)
        return jnp.where(cnt >= cap, cand, lo)

    thr = lax.fori_loop(0, 31, step, jnp.zeros((N_EXPERTS, 1), i32))
    n_gt = jnp.sum((bits > thr).astype(i32), axis=1, keepdims=True)
    thr_ref[...] = jnp.broadcast_to(thr, (N_EXPERTS, LANES))
    need_ref[...] = jnp.broadcast_to(cap - n_gt, (N_EXPERTS, LANES))


def _threshold(afft, cap):
    return pl.pallas_call(
        functools.partial(_threshold_kernel, cap=cap),
        out_shape=[jax.ShapeDtypeStruct((N_EXPERTS, LANES), i32), jax.ShapeDtypeStruct((N_EXPERTS, LANES), i32)],
        compiler_params=pltpu.CompilerParams(vmem_limit_bytes=VMEM_LIMIT),
        name="threshold",
    )(afft)


def _select_kernel(aff_ref, thr_ref, need_ref, rk_ref, tb_ref, carry_ref, *, n_sub):
    tt = SEL_TILE

    @pl.when(pl.program_id(0) == 0)
    def _():
        carry_ref[...] = jnp.zeros_like(carry_ref)

    lane_ok = lax.broadcasted_iota(i32, (tt, LANES), 1) < N_EXPERTS
    thr = thr_ref[...]
    need = need_ref[...].astype(f32)
    below = (lax.broadcasted_iota(i32, (tt, tt), 0) > lax.broadcasted_iota(i32, (tt, tt), 1)).astype(bf16)
    tie_carry = carry_ref[0:1, :]
    base = carry_ref[1:2, :]
    for k in range(n_sub):
        rows = slice(k * tt, (k + 1) * tt)
        bits = lax.bitcast_convert_type(aff_ref[rows, :], i32)
        gt = (bits > thr) & lane_ok
        eq = (bits == thr) & lane_ok
        tie_before = jnp.dot(below, eq.astype(bf16), preferred_element_type=f32) + tie_carry
        sel = gt | (eq & (tie_before < need))
        rank = jnp.dot(below, sel.astype(bf16), preferred_element_type=f32)
        n = jnp.sum(sel.astype(f32), axis=0, keepdims=True)
        rk_ref[rows, :] = jnp.where(sel, rank, -1.0)
        tb_ref[k, 0:1, :] = base.astype(i32)
        tb_ref[k, 1:2, :] = n.astype(i32)
        tb_ref[k, 2:8, :] = jnp.zeros((6, LANES), i32)
        tie_carry = tie_carry + jnp.sum(eq.astype(f32), axis=0, keepdims=True)
        base = base + n
    carry_ref[0:1, :] = tie_carry
    carry_ref[1:2, :] = base


def _select(aff, thr_row, need_row):
    t = aff.shape[0]
    nt = t // SEL_TILE
    n_sub = min(SELECT_SUB_TILES, nt)
    return pl.pallas_call(
        functools.partial(_select_kernel, n_sub=n_sub),
        grid=(nt // n_sub,),
        in_specs=[pl.BlockSpec((n_sub * SEL_TILE, LANES), lambda j: (j, 0)),
                  pl.BlockSpec((1, LANES), lambda j: (0, 0)),
                  pl.BlockSpec((1, LANES), lambda j: (0, 0))],
        out_specs=[pl.BlockSpec((n_sub * SEL_TILE, LANES), lambda j: (j, 0)),
                   pl.BlockSpec((n_sub, 8, LANES), lambda j: (j, 0, 0))],
        out_shape=[jax.ShapeDtypeStruct((t, LANES), f32), jax.ShapeDtypeStruct((nt, 8, LANES), i32)],
        scratch_shapes=[pltpu.VMEM((8, LANES), f32)],
        compiler_params=_cparams(("arbitrary",)),
        name="select",
    )(aff, thr_row, need_row)


WIN = SMALL_BUCKET + ROW_ALIGN
BIG_WIN = SEL_TILE + ROW_ALIGN


def _buckets(n):
    return ((WIN, (n > 0) & (n <= SMALL_BUCKET)), (BIG_WIN, n > SMALL_BUCKET))


def _aligned(x):
    return pl.multiple_of(x - x % ROW_ALIGN, ROW_ALIGN)


def _dispatch_kernel(base_ref, cnt_ref, fast_ref, rk_ref, hn_ref, xe_ref,
                     stage_ref, big_ref, carry_ref, sem_ref, bsem_ref, *, cap):
    j = pl.program_id(0)
    nt = pl.num_programs(0)
    tt = SEL_TILE
    ne = N_EXPERTS
    slot = j % 2

    @pl.when(j == 0)
    def _():
        carry_ref[...] = jnp.zeros_like(carry_ref)
        big_ref[0] = jnp.zeros((BIG_WIN, D_MODEL), bf16)
        tail = lambda e: pltpu.make_async_copy(big_ref.at[0], xe_ref.at[e, pl.ds(cap, BIG_WIN)], bsem_ref.at[e])
        for e in range(ne):
            tail(e).start()
        for e in range(ne):
            tail(e).wait()

    rkt = rk_ref[...].T
    hn = hn_ref[...]

    def targets(e):
        rem = base_ref[j * ne + e] % ROW_ALIGN
        row = rkt[e:e + 1, :]
        return jnp.where(row >= 0.0, row + rem.astype(f32), -1.0)

    def merge_carry(ref_rows, e):
        n = cnt_ref[j * ne + e]
        rem = base_ref[j * ne + e] % ROW_ALIGN
        head = ref_rows(0, ROW_ALIGN)
        head[...] += carry_ref[e]
        keep = pl.multiple_of(((rem + n) // ROW_ALIGN) * ROW_ALIGN, ROW_ALIGN)
        carry_ref[e] = ref_rows(keep, ROW_ALIGN)[...]

    def fast_copy(jj, sl, e):
        return pltpu.make_async_copy(stage_ref.at[sl, pl.ds(e * WIN, WIN)],
                                     xe_ref.at[e, pl.ds(_aligned(base_ref[jj * ne + e]), WIN)], sem_ref.at[e])

    def wait_previous():
        jp = jnp.maximum(j - 1, 0)

        @pl.when((j > 0) & (fast_ref[jp] == 1))
        def _():
            for e in range(ne):
                @pl.when(cnt_ref[jp * ne + e] > 0)
                def _(e=e):
                    fast_copy(jp, 1 - slot, e).wait()

    @pl.when(fast_ref[j] == 1)
    def _():
        win_slot = lax.broadcasted_iota(i32, (WIN, tt), 0).astype(f32)
        onehot = jnp.concatenate([(win_slot == targets(e)).astype(bf16) for e in range(ne)], axis=0)
        stage_ref[slot] = jnp.dot(onehot, hn, preferred_element_type=f32).astype(bf16)
        for e in range(ne):
            @pl.when(cnt_ref[j * ne + e] > 0)
            def _(e=e):
                merge_carry(lambda st, sz: stage_ref.at[slot, pl.ds(e * WIN + st, sz)], e)
        wait_previous()
        for e in range(ne):
            @pl.when(cnt_ref[j * ne + e] > 0)
            def _(e=e):
                fast_copy(j, slot, e).start()

        @pl.when(j == nt - 1)
        def _():
            for e in range(ne):
                @pl.when(cnt_ref[j * ne + e] > 0)
                def _(e=e):
                    fast_copy(j, slot, e).wait()

    @pl.when(fast_ref[j] == 0)
    def _():
        wait_previous()
        big_copy = lambda e, rows: pltpu.make_async_copy(
            big_ref.at[e, pl.ds(0, rows)], xe_ref.at[e, pl.ds(_aligned(base_ref[j * ne + e]), rows)], bsem_ref.at[e])
        for e in range(ne):
            for rows, cond in _buckets(cnt_ref[j * ne + e]):
                @pl.when(cond)
                def _(rows=rows, e=e):
                    win_slot = lax.broadcasted_iota(i32, (rows, tt), 0).astype(f32)
                    onehot = (win_slot == targets(e)).astype(bf16)
                    big_ref[e, 0:rows, :] = jnp.dot(onehot, hn, preferred_element_type=f32).astype(bf16)
                    merge_carry(lambda st, sz: big_ref.at[e, pl.ds(st, sz)], e)
                    big_copy(e, rows).start()
        for e in range(ne):
            for rows, cond in _buckets(cnt_ref[j * ne + e]):
                @pl.when(cond)
                def _(rows=rows, e=e):
                    big_copy(e, rows).wait()


def _dispatch(base, cnt, fast, rk, hn, cap):
    t = hn.shape[0]
    nt = t // SEL_TILE
    return pl.pallas_call(
        functools.partial(_dispatch_kernel, cap=cap),
        grid_spec=pltpu.PrefetchScalarGridSpec(
            num_scalar_prefetch=3,
            grid=(nt,),
            in_specs=[pl.BlockSpec((SEL_TILE, LANES), lambda j, b, c, f: (j, 0)),
                      pl.BlockSpec((SEL_TILE, D_MODEL), lambda j, b, c, f: (j, 0))],
            out_specs=pl.BlockSpec(memory_space=pl.ANY),
            scratch_shapes=[pltpu.VMEM((2, N_EXPERTS * WIN, D_MODEL), bf16),
                            pltpu.VMEM((N_EXPERTS, BIG_WIN, D_MODEL), bf16),
                            pltpu.VMEM((N_EXPERTS, ROW_ALIGN, D_MODEL), bf16),
                            pltpu.SemaphoreType.DMA((N_EXPERTS,)),
                            pltpu.SemaphoreType.DMA((N_EXPERTS,))],
        ),
        out_shape=jax.ShapeDtypeStruct((N_EXPERTS, cap + BIG_WIN, D_MODEL), bf16),
        compiler_params=_cparams(("arbitrary",)),
        name="dispatch",
    )(base, cnt, fast, rk, hn)


def _ffn_kernel(x_ref, wg_ref, wu_ref, wd_ref, y_ref):
    x = x_ref[0]
    g = jnp.dot(x, wg_ref[0], preferred_element_type=f32)
    u = jnp.dot(x, wu_ref[0], preferred_element_type=f32)
    h = (g * _sigmoid(g) * u).astype(bf16)
    y_ref[0] = jnp.dot(h, wd_ref[0], preferred_element_type=f32).astype(y_ref.dtype)


def _ffn(xe, wg, wu, wd, cap):
    tr = min(1024, cap)
    wspec = lambda a, bb: pl.BlockSpec((1, a, bb), lambda e, i: (e, 0, 0))
    return pl.pallas_call(
        _ffn_kernel,
        grid=(N_EXPERTS, cap // tr),
        in_specs=[pl.BlockSpec((1, tr, D_MODEL), lambda e, i: (e, i, 0)),
                  wspec(D_MODEL, D_EXPERT), wspec(D_MODEL, D_EXPERT), wspec(D_EXPERT, D_MODEL)],
        out_specs=pl.BlockSpec((1, tr, D_MODEL), lambda e, i: (e, i, 0)),
        out_shape=jax.ShapeDtypeStruct((N_EXPERTS, cap, D_MODEL), bf16),
        compiler_params=_cparams(("parallel", "parallel")),
        name="expert_ffn",
    )(xe, wg, wu, wd)


def _combine_kernel(base_ref, cnt_ref, fast_ref, rk_ref, aff_ref, x_ref, gf_ref, ye_ref, out_ref,
                    buf_ref, big_ref, sem_ref, bsem_ref, *, cap, final_norm):
    j = pl.program_id(0)
    nt = pl.num_programs(0)
    tt = SEL_TILE
    ne = N_EXPERTS
    slot = j % 2
    win = min(WIN, cap)
    window_start = lambda base, rows: pl.multiple_of(jnp.minimum(base - base % ROW_ALIGN, cap - rows), ROW_ALIGN)

    def win_copy(jj, sl, e):
        return pltpu.make_async_copy(ye_ref.at[e, pl.ds(window_start(base_ref[jj * ne + e], win), win)],
                                     buf_ref.at[sl, pl.ds(e * win, win)], sem_ref.at[sl])

    @pl.when((j == 0) & (fast_ref[0] == 1))
    def _():
        for e in range(ne):
            win_copy(0, 0, e).start()

    jn = jnp.minimum(j + 1, nt - 1)

    @pl.when((j + 1 < nt) & (fast_ref[jn] == 1))
    def _():
        for e in range(ne):
            win_copy(jn, 1 - slot, e).start()

    @pl.when(fast_ref[j] == 1)
    def _():
        rkt = rk_ref[...].T
        afft = aff_ref[...].T
        win_slot = lax.broadcasted_iota(i32, (win, tt), 0).astype(f32)
        his, los = [], []
        for e in range(ne):
            base = base_ref[j * ne + e]
            shift = (base - window_start(base, win)).astype(f32)
            row = rkt[e:e + 1, :]
            gate = jnp.where((win_slot == row + shift) & (row >= 0.0), afft[e:e + 1, :], 0.0)
            hi = gate.astype(bf16)
            his.append(hi)
            los.append((gate - hi.astype(f32)).astype(bf16))
        lhs = jnp.concatenate([jnp.concatenate(his, axis=0), jnp.concatenate(los, axis=0)], axis=1)
        for e in range(ne):
            win_copy(j, slot, e).wait()
        y = lax.dot_general(lhs, buf_ref[slot], (((0,), (0,)), ((), ())), preferred_element_type=f32)
        out_ref[...] = x_ref[...] + y[0:tt] + y[tt:2 * tt]

    @pl.when(fast_ref[j] == 0)
    def _():
        buckets = lambda n: tuple((min(rows, cap), cond) for rows, cond in _buckets(n))
        big_copy = lambda e, rows: pltpu.make_async_copy(
            ye_ref.at[e, pl.ds(window_start(base_ref[j * ne + e], rows), rows)], big_ref.at[e, pl.ds(0, rows)],
            bsem_ref.at[e])
        for e in range(ne):
            for rows, cond in buckets(cnt_ref[j * ne + e]):
                @pl.when(cond)
                def _(rows=rows, e=e):
                    big_copy(e, rows).start()
        out_ref[...] = x_ref[...]
        for e in range(ne):
            for rows, cond in buckets(cnt_ref[j * ne + e]):
                @pl.when(cond)
                def _(rows=rows, e=e):
                    base = base_ref[j * ne + e]
                    big_copy(e, rows).wait()
                    col = rk_ref[:, e:e + 1]
                    tgt = col + (base - window_start(base, rows)).astype(f32)
                    win_slot = lax.broadcasted_iota(i32, (tt, rows), 1).astype(f32)
                    onehot = ((tgt == win_slot) & (col >= 0.0)).astype(bf16)
                    contrib = jnp.dot(onehot, big_ref[e, 0:rows, :], preferred_element_type=f32)
                    out_ref[...] += contrib * aff_ref[:, e:e + 1]

    if final_norm:
        y = out_ref[...]
        ms = jnp.mean(y * y, axis=-1, keepdims=True)
        out_ref[...] = y * lax.rsqrt(ms + EPS) * gf_ref[...]


def _combine(base, cnt, fast, rk, aff, x2, gfin, ye, cap, final_norm):
    t = x2.shape[0]
    nt = t // SEL_TILE
    tile = lambda width: pl.BlockSpec((SEL_TILE, width), lambda j, b, c, f: (j, 0))
    return pl.pallas_call(
        functools.partial(_combine_kernel, cap=cap, final_norm=final_norm),
        grid_spec=pltpu.PrefetchScalarGridSpec(
            num_scalar_prefetch=3,
            grid=(nt,),
            in_specs=[tile(LANES), tile(LANES), tile(D_MODEL),
                      pl.BlockSpec((1, D_MODEL), lambda j, b, c, f: (0, 0)),
                      pl.BlockSpec(memory_space=pl.ANY)],
            out_specs=tile(D_MODEL),
            scratch_shapes=[pltpu.VMEM((2, N_EXPERTS * min(WIN, cap), D_MODEL), bf16),
                            pltpu.VMEM((N_EXPERTS, min(BIG_WIN, cap), D_MODEL), bf16),
                            pltpu.SemaphoreType.DMA((2,)),
                            pltpu.SemaphoreType.DMA((N_EXPERTS,))],
        ),
        out_shape=jax.ShapeDtypeStruct((t, D_MODEL), f32),
        compiler_params=_cparams(("arbitrary",)),
        name="combine",
    )(base, cnt, fast, rk, aff, x2, gfin, ye)


def _split_bf16(w):
    hi = w.astype(bf16)
    return jnp.concatenate([hi, (w - hi.astype(f32)).astype(bf16)], axis=1)


def _prep_layer(l, norm_mix, w_in, conv_w, a_log, dt_bias, a_norm, w_proj_a, w_proj_b, w_out,
                norm_ffn, w_router, w_gate, w_up, w_down):
    w = w_in[l]
    small0 = 4 * A_WIDTH
    small1 = small0 + 4 * A_HEADS
    qkvb1 = small1 + 3 * B_HEADS * HEAD_DIM
    qb1 = small1 + B_HEADS * HEAD_DIM
    w_main = jnp.concatenate([w[:, :small0], w[:, qkvb1:], w[:, small1:qb1] * (HEAD_DIM ** -0.5), w[:, qb1:qkvb1]],
                             axis=1).astype(bf16)
    w_small = jnp.pad(w[:, small0:small1], ((0, 0), (0, LANES - 4 * A_HEADS))).astype(bf16)
    par = jnp.zeros((8, LANES), f32)
    par = par.at[0, 2 * A_HEADS:4 * A_HEADS].set(jnp.exp(a_log[l].astype(f32)).reshape(-1))
    par = par.at[1, 2 * A_HEADS:4 * A_HEADS].set(dt_bias[l].astype(f32).reshape(-1))
    return dict(
        g_mix=norm_mix[l].reshape(1, D_MODEL), w_main=w_main, w_small=w_small, conv_w=conv_w[l], par=par,
        an=jnp.tile(a_norm[l], 2).reshape(1, LANES),
        wpa=w_proj_a[l].astype(bf16), wpb=w_proj_b[l].astype(bf16), wout=w_out[l].astype(bf16),
        g_ffn=norm_ffn[l].reshape(1, D_MODEL),
        wr=_split_bf16(jnp.pad(w_router[l], ((0, 0), (0, LANES - N_EXPERTS)))),
        wg=w_gate[l].astype(bf16), wu=w_up[l].astype(bf16), wd=w_down[l].astype(bf16),
    )


def _layer(x2, b, s, lw, gfin, final_norm):
    t = b * s
    main2, small2 = _inproj(x2, lw["g_mix"], lw["w_main"], lw["w_small"])
    main3 = main2.reshape(b, s, MAIN_WIDTH)
    qkvn, gb3, phase_views = _prep(main3, small2.reshape(b, s, LANES), lw["conv_w"], lw["par"])
    oa = _delta(qkvn, main3, gb3, lw["an"])
    obs, lses = [], []
    for g in range(len(B_GROUPS)):
        o_g, l_g = _attention(main3, g, phase_views.get(g))
        obs.append(o_g)
        lses.append(l_g)
    x2, hn, aff, afft = _merge_router(x2, oa.reshape(t, A_WIDTH), obs, lses, main2, lw["wpa"], lw["wpb"], lw["wout"],
                                      lw["g_ffn"], lw["wr"])
    cap = (CAPACITY_FACTOR * t) // N_EXPERTS
    thr, need = _threshold(afft, cap)
    pad_row = lambda a: jnp.pad(a[:, 0], (0, LANES - N_EXPERTS)).reshape(1, LANES)
    rk, tb = _select(aff, pad_row(thr), pad_row(need))
    base = tb[:, 0, :N_EXPERTS].reshape(-1)
    cnt = tb[:, 1, :N_EXPERTS].reshape(-1)
    fast = (jnp.max(tb[:, 1, :N_EXPERTS], axis=1) <= SMALL_BUCKET).astype(i32)
    xe = _dispatch(base, cnt, fast, rk, hn, cap)
    ye = _ffn(xe, lw["wg"], lw["wu"], lw["wd"], cap)
    return _combine(base, cnt, fast, rk, aff, x2, gfin, ye, cap, final_norm)


def kernel(x_prompt, x_sample, norm_mix, w_in, conv_w, a_log, dt_bias, a_norm, w_proj_a, w_proj_b, w_out,
           norm_ffn, w_router, w_gate, w_up, w_down, norm_final):
    depth = w_in.shape[0]
    layers = [_prep_layer(l, norm_mix, w_in, conv_w, a_log, dt_bias, a_norm, w_proj_a, w_proj_b, w_out,
                          norm_ffn, w_router, w_gate, w_up, w_down) for l in range(depth)]
    gfin = norm_final.reshape(1, D_MODEL)
    outs = []
    for x in (x_prompt, x_sample):
        b, s, d = x.shape
        x2 = x.reshape(b * s, d)
        for l in range(depth):
            x2 = _layer(x2, b, s, layers[l], gfin, l == depth - 1)
        outs.append(x2.reshape(b, s, d))
    return tuple(outs)
```

```python
import functools

import numpy as np
import jax
import jax.numpy as jnp
from jax import lax
from jax.experimental import pallas as pl
from jax.experimental.pallas import tpu as pltpu

f32 = jnp.float32
bf16 = jnp.bfloat16
i32 = jnp.int32
HIGHEST = lax.Precision.HIGHEST

D_MODEL = 1024
A_HEADS = 8
HEAD_DIM = 64
A_WIDTH = A_HEADS * HEAD_DIM
A_CONV = 5
CHUNK = 64
B_GROUPS = ((128, 1), (512, 4), (2048, 16))
B_HEADS_PER_GROUP = 4
B_HEADS = B_HEADS_PER_GROUP * len(B_GROUPS)
B_GROUP_WIDTH = B_HEADS_PER_GROUP * HEAD_DIM
B_SIDE = 64
N_EXPERTS = 16
D_EXPERT = 1024
CAPACITY_FACTOR = 2
EPS = 1e-6
NEG = -1e30

LANES = 128
MAIN_WIDTH = 2 * D_MODEL + 4 * A_WIDTH + 3 * B_HEADS * HEAD_DIM
Z_BLK = 3 * A_WIDTH // LANES
GATE_BLK = 4 * A_WIDTH // D_MODEL
QKV_B_BLK256 = (2 * D_MODEL + 4 * A_WIDTH) // B_GROUP_WIDTH
MAIN_BLK256 = MAIN_WIDTH // B_GROUP_WIDTH
N_TILE = 1280
SEL_TILE = 256
SELECT_SUB_TILES = 4
SMALL_BUCKET = 64
ROW_ALIGN = 16
VMEM_LIMIT = 56 * 1024 * 1024


def _cparams(sem):
    return pltpu.CompilerParams(dimension_semantics=sem, vmem_limit_bytes=VMEM_LIMIT)


def _sigmoid(x):
    return 1.0 / (1.0 + jnp.exp(-x))


def _softplus(x):
    return jnp.maximum(x, 0.0) + jnp.log(1.0 + jnp.exp(-jnp.abs(x)))


def _head_sumsq(y):
    sq = y * y
    hi = sq.astype(bf16)
    lo = (sq - hi.astype(f32)).astype(bf16)
    r = lax.broadcasted_iota(i32, (2 * LANES, LANES), 0) % LANES // HEAD_DIM
    c = lax.broadcasted_iota(i32, (2 * LANES, LANES), 1) // HEAD_DIM
    return jnp.dot(jnp.concatenate([hi, lo], axis=1), (r == c).astype(bf16), preferred_element_type=f32)


def _inproj_kernel(x_ref, g_ref, w_ref, ws_ref, o_ref, os_ref, n_ref):
    @pl.when(pl.program_id(1) == 0)
    def _():
        x = x_ref[...]
        ms = jnp.mean(x * x, axis=-1, keepdims=True)
        n = (x * lax.rsqrt(ms + EPS) * g_ref[...]).astype(bf16)
        n_ref[...] = n
        os_ref[...] = jnp.dot(n, ws_ref[...], preferred_element_type=f32)

    o_ref[...] = jnp.dot(n_ref[...], w_ref[...], preferred_element_type=f32).astype(o_ref.dtype)


def _inproj(x2, g, w_main, w_small):
    t = x2.shape[0]
    tm = min(1024, t)
    return pl.pallas_call(
        _inproj_kernel,
        grid=(t // tm, MAIN_WIDTH // N_TILE),
        in_specs=[
            pl.BlockSpec((tm, D_MODEL), lambda i, j: (i, 0)),
            pl.BlockSpec((1, D_MODEL), lambda i, j: (0, 0)),
            pl.BlockSpec((D_MODEL, N_TILE), lambda i, j: (0, j)),
            pl.BlockSpec((D_MODEL, LANES), lambda i, j: (0, 0)),
        ],
        out_specs=[
            pl.BlockSpec((tm, N_TILE), lambda i, j: (i, j)),
            pl.BlockSpec((tm, LANES), lambda i, j: (i, 0)),
        ],
        out_shape=[jax.ShapeDtypeStruct((t, MAIN_WIDTH), bf16), jax.ShapeDtypeStruct((t, LANES), f32)],
        scratch_shapes=[pltpu.VMEM((tm, D_MODEL), bf16)],
        compiler_params=_cparams(("parallel", "arbitrary")),
        name="inproj",
    )(x2, g, w_main, w_small)


HALO = 16
PREP_ROWS = 256
BETA_LANE = 0
CUM_LANE = 4 * A_HEADS


def _prep_kernel(cur_ref, prev_ref, next_ref, w_ref, sm_ref, par_ref, *rest, ts, n_tiles):
    dilated = [(g, dil) for g, (_, dil) in enumerate(B_GROUPS) if dil > 1]
    qkv_refs = rest[:3 * len(dilated)]
    o_ref, gb_ref = rest[3 * len(dilated):3 * len(dilated) + 2]
    ph_refs = rest[3 * len(dilated) + 2:3 * len(dilated) + 2 + len(dilated)]
    a_ref, s_ref = rest[-2:]
    i = pl.program_id(1)
    first = (i > 0).astype(f32)
    last = (i < n_tiles - 1).astype(f32)
    pad = (A_CONV - 1) // 2
    sub = min(PREP_ROWS, ts)
    for c in range(3 * A_WIDTH // LANES):
        cols = slice(c * LANES, (c + 1) * LANES)
        a_ref[0:HALO, :] = prev_ref[0, :, cols].astype(f32) * first
        a_ref[HALO:HALO + ts, :] = cur_ref[0, :, cols].astype(f32)
        a_ref[HALO + ts:2 * HALO + ts, :] = next_ref[0, :, cols].astype(f32) * last
        w = w_ref[:, cols]

        for k in range(ts // sub):
            r0 = k * sub
            y = jnp.zeros((sub, LANES), f32)
            for j in range(A_CONV):
                off = r0 + HALO - pad + j
                y = y + a_ref[off:off + sub, :] * w[j:j + 1]
            y = y * _sigmoid(y)
            if c < 2 * A_WIDTH // LANES:
                qscale = HEAD_DIM ** -0.5 if c < A_WIDTH // LANES else 1.0
                y = y * (lax.rsqrt(_head_sumsq(y) + EPS) * qscale)
            o_ref[0, r0:r0 + sub, cols] = y.astype(o_ref.dtype)

    ch = CHUNK
    ri = lax.broadcasted_iota(i32, (ch, ch), 0)
    ci = lax.broadcasted_iota(i32, (ch, ch), 1)
    lower = (ri >= ci).astype(f32)
    upper = (ri <= ci).astype(f32)
    lane = lax.broadcasted_iota(i32, (ch, LANES), 1)
    g_lane = 2 * A_HEADS
    is_fwd = lane < g_lane + A_HEADS
    for k in range(ts // ch):
        rows = slice(k * ch, (k + 1) * ch)
        sm = sm_ref[0, rows, :]
        g = -par_ref[0:1, :] * _softplus(sm + par_ref[1:2, :])
        cum = jnp.where(is_fwd,
                        jnp.dot(lower, g, preferred_element_type=f32, precision=HIGHEST),
                        jnp.dot(upper, g, preferred_element_type=f32, precision=HIGHEST))
        cum = pltpu.roll(cum, CUM_LANE - g_lane, axis=1)
        gb_ref[0, rows, :] = jnp.where(lane < g_lane, _sigmoid(sm), cum)

    w = B_GROUP_WIDTH
    for gi, (g, dil) in enumerate(dilated):
        for off in range(3):
            x = qkv_refs[3 * gi + off]
            for h in range(w // LANES):
                s_ref[h] = x[0, :, h * LANES:(h + 1) * LANES].astype(f32)
            for r in range(dil):
                for h in range(w // LANES):
                    c0 = (r * 3 + off) * w + h * LANES
                    ph_refs[gi][0, :, c0:c0 + LANES] = s_ref[h, pl.ds(r, ts // dil, stride=dil), :].astype(bf16)


def _prep(main3, small3, conv_w, par):
    b, s, _ = main3.shape
    ts = min(1024, s)
    n_tiles = s // ts
    hb = ts // HALO
    wa = 3 * A_WIDTH
    w = B_GROUP_WIDTH
    ngrp = len(B_GROUPS)
    dilated = [(g, dil) for g, (_, dil) in enumerate(B_GROUPS) if dil > 1]
    qkv_specs = [pl.BlockSpec((1, ts, w), functools.partial(lambda bi, i, blk: (bi, i, blk), blk=QKV_B_BLK256 + off * ngrp + g))
                 for g, _ in dilated for off in range(3)]
    outs = pl.pallas_call(
        functools.partial(_prep_kernel, ts=ts, n_tiles=n_tiles),
        grid=(b, n_tiles),
        in_specs=[
            pl.BlockSpec((1, ts, wa), lambda bi, i: (bi, i, 0)),
            pl.BlockSpec((1, HALO, wa), lambda bi, i: (bi, jnp.maximum(i * hb - 1, 0), 0)),
            pl.BlockSpec((1, HALO, wa), lambda bi, i: (bi, jnp.minimum((i + 1) * hb, s // HALO - 1), 0)),
            pl.BlockSpec((A_CONV, wa), lambda bi, i: (0, 0)),
            pl.BlockSpec((1, ts, LANES), lambda bi, i: (bi, i, 0)),
            pl.BlockSpec((8, LANES), lambda bi, i: (0, 0)),
        ] + qkv_specs,
        out_specs=[pl.BlockSpec((1, ts, wa), lambda bi, i: (bi, i, 0)),
                   pl.BlockSpec((1, ts, LANES), lambda bi, i: (bi, i, 0))]
        + [pl.BlockSpec((1, ts // dil, dil * 3 * w), lambda bi, i: (bi, i, 0)) for _, dil in dilated],
        out_shape=[jax.ShapeDtypeStruct((b, s, wa), bf16), jax.ShapeDtypeStruct((b, s, LANES), f32)]
        + [jax.ShapeDtypeStruct((b, s // dil, dil * 3 * w), bf16) for _, dil in dilated],
        scratch_shapes=[pltpu.VMEM((ts + 2 * HALO, LANES), f32), pltpu.VMEM((w // LANES, ts, LANES), f32)],
        compiler_params=_cparams(("parallel", "parallel")),
        name="prep",
    )(main3, main3, main3, conv_w, small3, par, *([main3] * len(qkv_specs)))
    return outs[0], outs[1], {g: ph for (g, _), ph in zip(dilated, outs[2:])}


GROUP = 8


def _split_heads(x, mask):
    z = jnp.zeros_like(x)
    return jnp.concatenate([jnp.where(mask, x, z), jnp.where(mask, z, x)], axis=0)


def _delta_stages(pair):
    c = CHUNK
    lane = lax.broadcasted_iota(i32, (c, LANES), 1)
    m0 = lane < HEAD_DIM
    m0w = jnp.concatenate([m0, m0], axis=1)
    rl = lax.broadcasted_iota(i32, (c, LANES), 0)
    cl = lane % HEAD_DIM
    eye_p = (rl == cl).astype(f32)
    same_head = (lax.broadcasted_iota(i32, (LANES, LANES), 0) // HEAD_DIM
                 == lax.broadcasted_iota(i32, (LANES, LANES), 1) // HEAD_DIM)
    bd = lambda x: _split_heads(x, m0).astype(bf16)
    bdw = lambda x: _split_heads(x, m0w).astype(bf16)

    def pick(full, base):
        c0 = jnp.sum(jnp.where(lane == base + 2 * pair, full, 0.0), axis=1, keepdims=True)
        c1 = jnp.sum(jnp.where(lane == base + 2 * pair + 1, full, 0.0), axis=1, keepdims=True)
        return jnp.where(m0, c0, c1)

    def gates(st):
        d = 1 if st["upper"] else 0
        st["beta"] = pick(st["gb"], BETA_LANE + d * A_HEADS)
        st["cum"] = pick(st["gb"], CUM_LANE + d * A_HEADS)

    def gram(st):
        q, k = st["q"], st["k"]
        st["gq"] = lax.dot_general(jnp.concatenate([q, k], axis=0), _split_heads(k, m0), (((1,), (1,)), ((), ())),
                                   preferred_element_type=f32)

    def decay(st):
        cum = st["cum"]
        cum_t = cum.T
        cum_row = jnp.concatenate([cum_t[0:1], cum_t[HEAD_DIM:HEAD_DIM + 1]], axis=1)
        incl = (rl <= cl) if st["upper"] else (rl >= cl)
        strict = (rl < cl) if st["upper"] else (rl > cl)
        dec = jnp.where(incl, jnp.exp(jnp.minimum(cum - cum_row, 0.0)), 0.0)
        gq = st.pop("gq")
        st["attn"] = (gq[0:c] * dec).astype(bf16)
        a = jnp.where(strict, -(st["beta"] * gq[c:2 * c] * dec), 0.0)
        st["p"] = eye_p + a
        st["a"] = jnp.dot(a.astype(bf16), bd(a), preferred_element_type=f32)

    def double(st):
        a, p = st["a"], st["p"]
        y = jnp.dot(jnp.concatenate([a, p], axis=0).astype(bf16), bd(a), preferred_element_type=f32)
        st["a"] = y[0:c]
        st["p"] = p + y[c:2 * c]

    def solve(st):
        a, p = st.pop("a"), st.pop("p")
        tinv = (p + jnp.dot(p.astype(bf16), bd(a), preferred_element_type=f32)).astype(bf16)
        cum = st["cum"]
        st["gtot"] = cum[0:1] if st["upper"] else cum[c - 1:c]
        st["eg"] = jnp.exp(cum)
        kf = st["k"].astype(f32)
        rhs = jnp.concatenate([st["v"].astype(f32) * st["beta"], kf * st["beta"] * st["eg"]], axis=1)
        st["uw"] = jnp.dot(tinv, bdw(rhs), preferred_element_type=f32)

    def finish(st):
        uw = st.pop("uw")
        aw = jnp.dot(st.pop("attn"), bdw(uw), preferred_element_type=f32)
        kd = (st["k"].astype(f32) * jnp.exp(st["gtot"] - st["cum"])).astype(bf16)
        t = lax.dot_general(kd, uw.astype(bf16), (((0,), (0,)), ((), ())), preferred_element_type=f32)
        st["oin"] = aw[:, 0:LANES]
        st["qeff"] = (st["q"].astype(f32) * st["eg"] - aw[:, LANES:2 * LANES]).astype(bf16)
        st["bbd"] = jnp.where(same_head, t[:, 0:LANES], 0.0)
        st["abd"] = jnp.where(same_head, -t[:, LANES:2 * LANES], 0.0).astype(bf16)
        st["dec"] = jnp.exp(st["gtot"])

    n_double = int(np.log2(c)) - 2
    return [gates, gram, decay] + [double] * n_double + [solve, finish]


def _delta_kernel(q_ref, k_ref, v_ref, z_ref, gb_ref, an_ref, o_ref,
                  acc_ref, st_ref, qe_ref, oi_ref, ab_ref, bb_ref, dc_ref, *, s):
    c = CHUNK
    n = s // c
    g = min(GROUP, n)
    ng = n // g
    pair = pl.program_id(1)
    acc_ref[...] = jnp.zeros_like(acc_ref)
    st_ref[...] = jnp.zeros_like(st_ref)
    stages = _delta_stages(pair)

    def row_start(gi, t, d):
        cidx = gi * g + t if d == 0 else n - 1 - (gi * g + t)
        return pl.multiple_of(cidx * c, c)

    def group_step(gi_a, slot_a, gi_b, slot_b):
        streams = []
        if gi_a is not None:
            for t in range(g):
                for d in range(2):
                    r0 = row_start(gi_a, t, d)
                    streams.append(dict(
                        q=q_ref[0, pl.ds(r0, c), :], k=k_ref[0, pl.ds(r0, c), :], v=v_ref[0, pl.ds(r0, c), :],
                        gb=gb_ref[0, pl.ds(r0, c), :], upper=(d == 1), idx=(d * 2 + slot_a) * g + t))
        states = [st_ref[0], st_ref[1]] if gi_b is not None else None

        def recurrence(t):
            for d in range(2):
                r0 = row_start(gi_b, t, d)
                idx = (d * 2 + slot_b) * g + t
                y = jnp.dot(jnp.concatenate([ab_ref[idx], qe_ref[idx]], axis=0), states[d].astype(bf16),
                            preferred_element_type=f32)
                acc_ref[pl.ds(r0, c), :] += y[LANES:LANES + c] + oi_ref[idx]
                states[d] = states[d] * dc_ref[idx][0:1] + y[0:LANES] + bb_ref[idx]

        done_b = 0
        for si, stage in enumerate(stages):
            for st in streams:
                stage(st)
            if gi_b is not None and si >= 1 and done_b < g:
                recurrence(done_b)
                done_b += 1
        if gi_b is not None:
            for t in range(done_b, g):
                recurrence(t)
            st_ref[0] = states[0]
            st_ref[1] = states[1]
        for st in streams:
            idx = st["idx"]
            qe_ref[idx] = st["qeff"]
            oi_ref[idx] = st["oin"]
            ab_ref[idx] = st["abd"]
            bb_ref[idx] = st["bbd"]
            dc_ref[idx] = jnp.broadcast_to(st["dec"], (8, LANES))

    group_step(0, 0, None, None)

    def body(i, carry):
        group_step(i, i % 2, i - 1, (i - 1) % 2)
        return carry

    lax.fori_loop(1, ng, body, 0)
    group_step(None, None, ng - 1, (ng - 1) % 2)

    rows = min(256, s)

    def epi(i, carry):
        r0 = pl.multiple_of(i * rows, rows)
        o = acc_ref[pl.ds(r0, rows), :]
        ms = _head_sumsq(o) * (1.0 / HEAD_DIM)
        z = z_ref[0, pl.ds(r0, rows), :].astype(f32)
        y = o * lax.rsqrt(ms + EPS) * an_ref[...] * (z * _sigmoid(z))
        o_ref[0, pl.ds(r0, rows), :] = y.astype(o_ref.dtype)
        return carry

    lax.fori_loop(0, s // rows, epi, 0)


def _delta(qkvn, main3, gb3, an):
    b, s, _ = qkvn.shape
    npair = A_HEADS // 2
    kb = A_WIDTH // LANES
    nbuf = 4 * min(GROUP, s // CHUNK)
    seq = lambda off: pl.BlockSpec((1, s, LANES), lambda bi, p: (bi, 0, off + p))
    return pl.pallas_call(
        functools.partial(_delta_kernel, s=s),
        grid=(b, npair),
        in_specs=[
            seq(0), seq(kb), seq(2 * kb), seq(Z_BLK),
            pl.BlockSpec((1, s, LANES), lambda bi, p: (bi, 0, 0)),
            pl.BlockSpec((1, LANES), lambda bi, p: (0, 0)),
        ],
        out_specs=pl.BlockSpec((1, s, LANES), lambda bi, p: (bi, 0, p)),
        out_shape=jax.ShapeDtypeStruct((b, s, A_WIDTH), bf16),
        scratch_shapes=[pltpu.VMEM((s, LANES), f32), pltpu.VMEM((2, LANES, LANES), f32),
                        pltpu.VMEM((nbuf, CHUNK, LANES), bf16), pltpu.VMEM((nbuf, CHUNK, LANES), f32),
                        pltpu.VMEM((nbuf, LANES, LANES), bf16), pltpu.VMEM((nbuf, LANES, LANES), f32),
                        pltpu.VMEM((nbuf, 8, LANES), f32)],
        compiler_params=_cparams(("parallel", "arbitrary")),
        name="delta",
    )(qkvn, qkvn, qkvn, main3, gb3, an)


def _attn_kernel(*refs, lp, bq, nk, dil, slopes, nph, fused):
    w = B_GROUP_WIDTH
    nh = B_HEADS_PER_GROUP
    if fused:
        x_ref, o_ref, l_ref, bias_ref = refs
        src = lambda j, off: (x_ref, (j * 3 + off) * w)
    else:
        q_ref, k_ref, v_ref, o_ref, l_ref, bias_ref = refs
        src = lambda j, off: ((q_ref, k_ref, v_ref)[off], 0)
    head_of_lane = lax.broadcasted_iota(i32, (1, w), 1) // HEAD_DIM
    nq = lp // bq
    row = lax.broadcasted_iota(i32, (bq, nk), 0)
    col = lax.broadcasted_iota(i32, (bq, nk), 1)

    def alibi_bias(offset):
        adelta = jnp.abs(col - row + offset)
        dist = adelta.astype(f32) * float(dil)
        return [jnp.where(adelta <= B_SIDE, -slopes[h] * dist, NEG) for h in range(nh)]

    interior = -B_SIDE
    has_interior = nq > 2 and nk == bq + 2 * B_SIDE
    if has_interior:
        for h, b in enumerate(alibi_bias(interior)):
            bias_ref[h] = b

    def load(j, off, start, size):
        ref, lane0 = src(j, off)
        return ref[0, pl.ds(start, size), lane0:lane0 + w]

    def blocks(items, hoisted):
        work = []
        for j, i in items:
            q0 = pl.multiple_of(i * bq, bq)
            ks = pl.multiple_of(jnp.clip(q0 - B_SIDE, 0, lp - nk), B_SIDE)
            q = load(j, 0, q0, bq)
            zq = jnp.zeros_like(q)
            qs = jnp.concatenate([jnp.where(head_of_lane == h, q, zq) for h in range(nh)], axis=0)
            sc = lax.dot_general(qs, load(j, 1, ks, nk), (((1,), (1,)), ((), ())), preferred_element_type=f32)
            bias = [bias_ref[h] for h in range(nh)] if hoisted else alibi_bias(ks - q0)
            work.append(dict(j=j, q0=q0, ks=ks, sc=sc, bias=bias,
                             o=jnp.zeros((bq, w), f32), lse=jnp.zeros((bq, w), f32)))
        for h in range(nh):
            hm = head_of_lane == h
            for wk in work:
                sh = wk["sc"][h * bq:(h + 1) * bq] + wk["bias"][h]
                m = jnp.max(sh, axis=1, keepdims=True)
                p = jnp.exp(sh - m)
                l = jnp.sum(p, axis=1, keepdims=True)
                oh = jnp.dot(p.astype(bf16), load(wk["j"], 2, wk["ks"], nk), preferred_element_type=f32) / l
                wk["o"] = jnp.where(hm, oh, wk["o"])
                wk["lse"] = jnp.where(hm, m + jnp.log(l), wk["lse"])
        for wk in work:
            lanes = slice(wk["j"] * w, (wk["j"] + 1) * w)
            o_ref[0, pl.ds(wk["q0"], bq), lanes] = wk["o"].astype(o_ref.dtype)
            l_ref[0, pl.ds(wk["q0"], bq), lanes] = wk["lse"]

    if nq == 1:
        for j in range(0, nph, 2):
            blocks([(jj, 0) for jj in range(j, min(j + 2, nph))], False)
        return
    assert nq % 2 == 0
    npair = nq // 2
    for j in range(nph):
        def body(pi, carry, j=j):
            items = [(j, 2 * pi), (j, 2 * pi + 1)]
            if has_interior:
                is_interior = (pi > 0) & (pi < npair - 1)

                @pl.when(is_interior)
                def _():
                    blocks(items, True)

                @pl.when(jnp.logical_not(is_interior))
                def _():
                    blocks(items, False)
            else:
                blocks(items, False)
            return carry

        lax.fori_loop(0, npair, body, 0)


ATTN_ROWS = 4096


def _attention(main3, group, phase_view):
    b, s, _ = main3.shape
    _, dil = B_GROUPS[group]
    lp = s // dil
    bq = min(128, lp)
    nk = min(bq + 2 * B_SIDE, lp)
    slopes = tuple(float(2.0 ** (-8.0 * (group * B_HEADS_PER_GROUP + h + 1) / B_HEADS))
                   for h in range(B_HEADS_PER_GROUP))
    w = B_GROUP_WIDTH
    ngrp = len(B_GROUPS)
    fused = dil > 1
    nph = max(1, min(dil, ATTN_ROWS // lp)) if fused else 1
    if fused:
        srcs = [phase_view]
        in_specs = [pl.BlockSpec((1, lp, nph * 3 * w), lambda bi, r: (bi, 0, r))]
    else:
        srcs = [main3] * 3
        in_specs = [pl.BlockSpec((1, lp, w), functools.partial(lambda bi, r, blk: (bi, 0, blk),
                                                                blk=QKV_B_BLK256 + off * ngrp + group))
                    for off in range(3)]
    o, lse = pl.pallas_call(
        functools.partial(_attn_kernel, lp=lp, bq=bq, nk=nk, dil=dil, slopes=slopes, nph=nph, fused=fused),
        grid=(b, dil // nph),
        in_specs=in_specs,
        out_specs=[pl.BlockSpec((1, lp, nph * w), lambda bi, r: (bi, 0, r)),
                   pl.BlockSpec((1, lp, nph * w), lambda bi, r: (bi, 0, r))],
        out_shape=[jax.ShapeDtypeStruct((b, lp, dil * w), bf16), jax.ShapeDtypeStruct((b, lp, dil * w), f32)],
        scratch_shapes=[pltpu.VMEM((B_HEADS_PER_GROUP, bq, nk), f32)],
        compiler_params=_cparams(("parallel", "parallel")),
        name=f"attn_d{dil}",
    )(*srcs)
    return o.reshape(b * s, w), lse.reshape(b * s, w)


def _merge_kernel(x_ref, oa_ref, o1_ref, o2_ref, o3_ref, l1_ref, l2_ref, l3_ref, ga_ref, gb_ref,
                  wpa_ref, wpb_ref, wout_ref, g_ref, wr_ref, out_ref, hn_ref, aff_ref, afft_ref):
    l1, l2, l3 = l1_ref[...], l2_ref[...], l3_ref[...]
    m = jnp.maximum(jnp.maximum(l1, l2), l3)
    e1, e2, e3 = jnp.exp(l1 - m), jnp.exp(l2 - m), jnp.exp(l3 - m)
    ob = (e1 * o1_ref[...].astype(f32) + e2 * o2_ref[...].astype(f32) + e3 * o3_ref[...].astype(f32)) / (e1 + e2 + e3)
    ya = jnp.dot(oa_ref[...], wpa_ref[...], preferred_element_type=f32)
    yb = jnp.dot(ob.astype(bf16), wpb_ref[...], preferred_element_type=f32)
    mix = _sigmoid(ga_ref[...].astype(f32)) * ya + _sigmoid(gb_ref[...].astype(f32)) * yb
    x = x_ref[...] + jnp.dot(mix.astype(bf16), wout_ref[...], preferred_element_type=f32)
    out_ref[...] = x

    ms = jnp.mean(x * x, axis=-1, keepdims=True)
    hn = x * lax.rsqrt(ms + EPS) * g_ref[...]
    hn_ref[...] = hn.astype(bf16)
    hi = hn.astype(bf16)
    lo = (hn - hi.astype(f32)).astype(bf16)
    tm = hn.shape[0]
    prod = jnp.dot(jnp.concatenate([hi, lo], axis=0), wr_ref[...], preferred_element_type=f32)
    logits = (prod[0:tm, 0:LANES] + prod[0:tm, LANES:2 * LANES]) + prod[tm:2 * tm, 0:LANES]
    lane = lax.broadcasted_iota(i32, logits.shape, 1)
    logits = jnp.where(lane < N_EXPERTS, logits, NEG)
    mx = jnp.max(logits, axis=1, keepdims=True)
    e = jnp.exp(logits - mx)
    aff = e / jnp.sum(e, axis=1, keepdims=True)
    aff_ref[...] = aff
    afft_ref[...] = aff.T[0:N_EXPERTS]


def _merge_router(x2, oa2, obs, lses, main2, wpa, wpb, wout, g_ffn, wr):
    t = x2.shape[0]
    tm = min(512, t)
    w = B_GROUP_WIDTH
    row = lambda width: pl.BlockSpec((tm, width), lambda i: (i, 0))
    full = lambda a, bb: pl.BlockSpec((a, bb), lambda i: (0, 0))
    return pl.pallas_call(
        _merge_kernel,
        grid=(t // tm,),
        in_specs=[row(D_MODEL), row(A_WIDTH), row(w), row(w), row(w), row(w), row(w), row(w),
                  pl.BlockSpec((tm, D_MODEL), lambda i: (i, GATE_BLK)),
                  pl.BlockSpec((tm, D_MODEL), lambda i: (i, GATE_BLK + 1)),
                  full(A_WIDTH, D_MODEL), full(w, D_MODEL), full(D_MODEL, D_MODEL),
                  full(1, D_MODEL), full(D_MODEL, 2 * LANES)],
        out_specs=[row(D_MODEL), row(D_MODEL), row(LANES), pl.BlockSpec((N_EXPERTS, tm), lambda i: (0, i))],
        out_shape=[jax.ShapeDtypeStruct((t, D_MODEL), f32), jax.ShapeDtypeStruct((t, D_MODEL), bf16),
                   jax.ShapeDtypeStruct((t, LANES), f32), jax.ShapeDtypeStruct((N_EXPERTS, t), f32)],
        compiler_params=_cparams(("parallel",)),
        name="merge_router",
    )(x2, oa2, *obs, *lses, main2, main2, wpa, wpb, wout, g_ffn, wr)


def _threshold_kernel(afft_ref, thr_ref, need_ref, *, cap):
    bits = lax.bitcast_convert_type(afft_ref[...], i32)

    def step(i, lo):
        cand = lo | lax.shift_left(jnp.int32(1), 30 - i)
        cnt = jnp.sum((bits >= cand).astype(i32), axis=1, keepdims=True)
        return jnp.where(cnt >= cap, cand, lo)

    thr = lax.fori_loop(0, 31, step, jnp.zeros((N_EXPERTS, 1), i32))
    n_gt = jnp.sum((bits > thr).astype(i32), axis=1, keepdims=True)
    thr_ref[...] = jnp.broadcast_to(thr, (N_EXPERTS, LANES))
    need_ref[...] = jnp.broadcast_to(cap - n_gt, (N_EXPERTS, LANES))


def _threshold(afft, cap):
    return pl.pallas_call(
        functools.partial(_threshold_kernel, cap=cap),
        out_shape=[jax.ShapeDtypeStruct((N_EXPERTS, LANES), i32), jax.ShapeDtypeStruct((N_EXPERTS, LANES), i32)],
        compiler_params=pltpu.CompilerParams(vmem_limit_bytes=VMEM_LIMIT),
        name="threshold",
    )(afft)


def _select_kernel(aff_ref, thr_ref, need_ref, rk_ref, tb_ref, carry_ref, *, n_sub):
    tt = SEL_TILE

    @pl.when(pl.program_id(0) == 0)
    def _():
        carry_ref[...] = jnp.zeros_like(carry_ref)

    lane_ok = lax.broadcasted_iota(i32, (tt, LANES), 1) < N_EXPERTS
    thr = thr_ref[...]
    need = need_ref[...].astype(f32)
    below = (lax.broadcasted_iota(i32, (tt, tt), 0) > lax.broadcasted_iota(i32, (tt, tt), 1)).astype(bf16)
    tie_carry = carry_ref[0:1, :]
    base = carry_ref[1:2, :]
    for k in range(n_sub):
        rows = slice(k * tt, (k + 1) * tt)
        bits = lax.bitcast_convert_type(aff_ref[rows, :], i32)
        gt = (bits > thr) & lane_ok
        eq = (bits == thr) & lane_ok
        tie_before = jnp.dot(below, eq.astype(bf16), preferred_element_type=f32) + tie_carry
        sel = gt | (eq & (tie_before < need))
        rank = jnp.dot(below, sel.astype(bf16), preferred_element_type=f32)
        n = jnp.sum(sel.astype(f32), axis=0, keepdims=True)
        rk_ref[rows, :] = jnp.where(sel, rank, -1.0)
        tb_ref[k, 0:1, :] = base.astype(i32)
        tb_ref[k, 1:2, :] = n.astype(i32)
        tb_ref[k, 2:8, :] = jnp.zeros((6, LANES), i32)
        tie_carry = tie_carry + jnp.sum(eq.astype(f32), axis=0, keepdims=True)
        base = base + n
    carry_ref[0:1, :] = tie_carry
    carry_ref[1:2, :] = base


def _select(aff, thr_row, need_row):
    t = aff.shape[0]
    nt = t // SEL_TILE
    n_sub = min(SELECT_SUB_TILES, nt)
    return pl.pallas_call(
        functools.partial(_select_kernel, n_sub=n_sub),
        grid=(nt // n_sub,),
        in_specs=[pl.BlockSpec((n_sub * SEL_TILE, LANES), lambda j: (j, 0)),
                  pl.BlockSpec((1, LANES), lambda j: (0, 0)),
                  pl.BlockSpec((1, LANES), lambda j: (0, 0))],
        out_specs=[pl.BlockSpec((n_sub * SEL_TILE, LANES), lambda j: (j, 0)),
                   pl.BlockSpec((n_sub, 8, LANES), lambda j: (j, 0, 0))],
        out_shape=[jax.ShapeDtypeStruct((t, LANES), f32), jax.ShapeDtypeStruct((nt, 8, LANES), i32)],
        scratch_shapes=[pltpu.VMEM((8, LANES), f32)],
        compiler_params=_cparams(("arbitrary",)),
        name="select",
    )(aff, thr_row, need_row)


WIN = SMALL_BUCKET + ROW_ALIGN
BIG_WIN = SEL_TILE + ROW_ALIGN


def _buckets(n):
    return ((WIN, (n > 0) & (n <= SMALL_BUCKET)), (BIG_WIN, n > SMALL_BUCKET))


def _aligned(x):
    return pl.multiple_of(x - x % ROW_ALIGN, ROW_ALIGN)


def _dispatch_kernel(base_ref, cnt_ref, fast_ref, rk_ref, hn_ref, xe_ref,
                     stage_ref, big_ref, carry_ref, sem_ref, bsem_ref, *, cap):
    j = pl.program_id(0)
    nt = pl.num_programs(0)
    tt = SEL_TILE
    ne = N_EXPERTS
    slot = j % 2

    @pl.when(j == 0)
    def _():
        carry_ref[...] = jnp.zeros_like(carry_ref)
        big_ref[0] = jnp.zeros((BIG_WIN, D_MODEL), bf16)
        tail = lambda e: pltpu.make_async_copy(big_ref.at[0], xe_ref.at[e, pl.ds(cap, BIG_WIN)], bsem_ref.at[e])
        for e in range(ne):
            tail(e).start()
        for e in range(ne):
            tail(e).wait()

    rkt = rk_ref[...].T
    hn = hn_ref[...]

    def targets(e):
        rem = base_ref[j * ne + e] % ROW_ALIGN
        row = rkt[e:e + 1, :]
        return jnp.where(row >= 0.0, row + rem.astype(f32), -1.0)

    def merge_carry(ref_rows, e):
        n = cnt_ref[j * ne + e]
        rem = base_ref[j * ne + e] % ROW_ALIGN
        head = ref_rows(0, ROW_ALIGN)
        head[...] += carry_ref[e]
        keep = pl.multiple_of(((rem + n) // ROW_ALIGN) * ROW_ALIGN, ROW_ALIGN)
        carry_ref[e] = ref_rows(keep, ROW_ALIGN)[...]

    def fast_copy(jj, sl, e):
        return pltpu.make_async_copy(stage_ref.at[sl, pl.ds(e * WIN, WIN)],
                                     xe_ref.at[e, pl.ds(_aligned(base_ref[jj * ne + e]), WIN)], sem_ref.at[e])

    def wait_previous():
        jp = jnp.maximum(j - 1, 0)

        @pl.when((j > 0) & (fast_ref[jp] == 1))
        def _():
            for e in range(ne):
                @pl.when(cnt_ref[jp * ne + e] > 0)
                def _(e=e):
                    fast_copy(jp, 1 - slot, e).wait()

    @pl.when(fast_ref[j] == 1)
    def _():
        win_slot = lax.broadcasted_iota(i32, (WIN, tt), 0).astype(f32)
        onehot = jnp.concatenate([(win_slot == targets(e)).astype(bf16) for e in range(ne)], axis=0)
        stage_ref[slot] = jnp.dot(onehot, hn, preferred_element_type=f32).astype(bf16)
        for e in range(ne):
            @pl.when(cnt_ref[j * ne + e] > 0)
            def _(e=e):
                merge_carry(lambda st, sz: stage_ref.at[slot, pl.ds(e * WIN + st, sz)], e)
        wait_previous()
        for e in range(ne):
            @pl.when(cnt_ref[j * ne + e] > 0)
            def _(e=e):
                fast_copy(j, slot, e).start()

        @pl.when(j == nt - 1)
        def _():
            for e in range(ne):
                @pl.when(cnt_ref[j * ne + e] > 0)
                def _(e=e):
                    fast_copy(j, slot, e).wait()

    @pl.when(fast_ref[j] == 0)
    def _():
        wait_previous()
        big_copy = lambda e, rows: pltpu.make_async_copy(
            big_ref.at[e, pl.ds(0, rows)], xe_ref.at[e, pl.ds(_aligned(base_ref[j * ne + e]), rows)], bsem_ref.at[e])
        for e in range(ne):
            for rows, cond in _buckets(cnt_ref[j * ne + e]):
                @pl.when(cond)
                def _(rows=rows, e=e):
                    win_slot = lax.broadcasted_iota(i32, (rows, tt), 0).astype(f32)
                    onehot = (win_slot == targets(e)).astype(bf16)
                    big_ref[e, 0:rows, :] = jnp.dot(onehot, hn, preferred_element_type=f32).astype(bf16)
                    merge_carry(lambda st, sz: big_ref.at[e, pl.ds(st, sz)], e)
                    big_copy(e, rows).start()
        for e in range(ne):
            for rows, cond in _buckets(cnt_ref[j * ne + e]):
                @pl.when(cond)
                def _(rows=rows, e=e):
                    big_copy(e, rows).wait()


def _dispatch(base, cnt, fast, rk, hn, cap):
    t = hn.shape[0]
    nt = t // SEL_TILE
    return pl.pallas_call(
        functools.partial(_dispatch_kernel, cap=cap),
        grid_spec=pltpu.PrefetchScalarGridSpec(
            num_scalar_prefetch=3,
            grid=(nt,),
            in_specs=[pl.BlockSpec((SEL_TILE, LANES), lambda j, b, c, f: (j, 0)),
                      pl.BlockSpec((SEL_TILE, D_MODEL), lambda j, b, c, f: (j, 0))],
            out_specs=pl.BlockSpec(memory_space=pl.ANY),
            scratch_shapes=[pltpu.VMEM((2, N_EXPERTS * WIN, D_MODEL), bf16),
                            pltpu.VMEM((N_EXPERTS, BIG_WIN, D_MODEL), bf16),
                            pltpu.VMEM((N_EXPERTS, ROW_ALIGN, D_MODEL), bf16),
                            pltpu.SemaphoreType.DMA((N_EXPERTS,)),
                            pltpu.SemaphoreType.DMA((N_EXPERTS,))],
        ),
        out_shape=jax.ShapeDtypeStruct((N_EXPERTS, cap + BIG_WIN, D_MODEL), bf16),
        compiler_params=_cparams(("arbitrary",)),
        name="dispatch",
    )(base, cnt, fast, rk, hn)


def _ffn_kernel(x_ref, wg_ref, wu_ref, wd_ref, y_ref):
    x = x_ref[0]
    g = jnp.dot(x, wg_ref[0], preferred_element_type=f32)
    u = jnp.dot(x, wu_ref[0], preferred_element_type=f32)
    h = (g * _sigmoid(g) * u).astype(bf16)
    y_ref[0] = jnp.dot(h, wd_ref[0], preferred_element_type=f32).astype(y_ref.dtype)


def _ffn(xe, wg, wu, wd, cap):
    tr = min(1024, cap)
    wspec = lambda a, bb: pl.BlockSpec((1, a, bb), lambda e, i: (e, 0, 0))
    return pl.pallas_call(
        _ffn_kernel,
        grid=(N_EXPERTS, cap // tr),
        in_specs=[pl.BlockSpec((1, tr, D_MODEL), lambda e, i: (e, i, 0)),
                  wspec(D_MODEL, D_EXPERT), wspec(D_MODEL, D_EXPERT), wspec(D_EXPERT, D_MODEL)],
        out_specs=pl.BlockSpec((1, tr, D_MODEL), lambda e, i: (e, i, 0)),
        out_shape=jax.ShapeDtypeStruct((N_EXPERTS, cap, D_MODEL), bf16),
        compiler_params=_cparams(("parallel", "parallel")),
        name="expert_ffn",
    )(xe, wg, wu, wd)


def _combine_kernel(base_ref, cnt_ref, fast_ref, rk_ref, aff_ref, x_ref, gf_ref, ye_ref, out_ref,
                    buf_ref, big_ref, sem_ref, bsem_ref, *, cap, final_norm):
    j = pl.program_id(0)
    nt = pl.num_programs(0)
    tt = SEL_TILE
    ne = N_EXPERTS
    slot = j % 2
    win = min(WIN, cap)
    window_start = lambda base, rows: pl.multiple_of(jnp.minimum(base - base % ROW_ALIGN, cap - rows), ROW_ALIGN)

    def win_copy(jj, sl, e):
        return pltpu.make_async_copy(ye_ref.at[e, pl.ds(window_start(base_ref[jj * ne + e], win), win)],
                                     buf_ref.at[sl, pl.ds(e * win, win)], sem_ref.at[sl])

    @pl.when((j == 0) & (fast_ref[0] == 1))
    def _():
        for e in range(ne):
            win_copy(0, 0, e).start()

    jn = jnp.minimum(j + 1, nt - 1)

    @pl.when((j + 1 < nt) & (fast_ref[jn] == 1))
    def _():
        for e in range(ne):
            win_copy(jn, 1 - slot, e).start()

    @pl.when(fast_ref[j] == 1)
    def _():
        rkt = rk_ref[...].T
        afft = aff_ref[...].T
        win_slot = lax.broadcasted_iota(i32, (win, tt), 0).astype(f32)
        his, los = [], []
        for e in range(ne):
            base = base_ref[j * ne + e]
            shift = (base - window_start(base, win)).astype(f32)
            row = rkt[e:e + 1, :]
            gate = jnp.where((win_slot == row + shift) & (row >= 0.0), afft[e:e + 1, :], 0.0)
            hi = gate.astype(bf16)
            his.append(hi)
            los.append((gate - hi.astype(f32)).astype(bf16))
        lhs = jnp.concatenate([jnp.concatenate(his, axis=0), jnp.concatenate(los, axis=0)], axis=1)
        for e in range(ne):
            win_copy(j, slot, e).wait()
        y = lax.dot_general(lhs, buf_ref[slot], (((0,), (0,)), ((), ())), preferred_element_type=f32)
        out_ref[...] = x_ref[...] + y[0:tt] + y[tt:2 * tt]

    @pl.when(fast_ref[j] == 0)
    def _():
        buckets = lambda n: tuple((min(rows, cap), cond) for rows, cond in _buckets(n))
        big_copy = lambda e, rows: pltpu.make_async_copy(
            ye_ref.at[e, pl.ds(window_start(base_ref[j * ne + e], rows), rows)], big_ref.at[e, pl.ds(0, rows)],
            bsem_ref.at[e])
        for e in range(ne):
            for rows, cond in buckets(cnt_ref[j * ne + e]):
                @pl.when(cond)
                def _(rows=rows, e=e):
                    big_copy(e, rows).start()
        out_ref[...] = x_ref[...]
        for e in range(ne):
            for rows, cond in buckets(cnt_ref[j * ne + e]):
                @pl.when(cond)
                def _(rows=rows, e=e):
                    base = base_ref[j * ne + e]
                    big_copy(e, rows).wait()
                    col = rk_ref[:, e:e + 1]
                    tgt = col + (base - window_start(base, rows)).astype(f32)
                    win_slot = lax.broadcasted_iota(i32, (tt, rows), 1).astype(f32)
                    onehot = ((tgt == win_slot) & (col >= 0.0)).astype(bf16)
                    contrib = jnp.dot(onehot, big_ref[e, 0:rows, :], preferred_element_type=f32)
                    out_ref[...] += contrib * aff_ref[:, e:e + 1]

    if final_norm:
        y = out_ref[...]
        ms = jnp.mean(y * y, axis=-1, keepdims=True)
        out_ref[...] = y * lax.rsqrt(ms + EPS) * gf_ref[...]


def _combine(base, cnt, fast, rk, aff, x2, gfin, ye, cap, final_norm):
    t = x2.shape[0]
    nt = t // SEL_TILE
    tile = lambda width: pl.BlockSpec((SEL_TILE, width), lambda j, b, c, f: (j, 0))
    return pl.pallas_call(
        functools.partial(_combine_kernel, cap=cap, final_norm=final_norm),
        grid_spec=pltpu.PrefetchScalarGridSpec(
            num_scalar_prefetch=3,
            grid=(nt,),
            in_specs=[tile(LANES), tile(LANES), tile(D_MODEL),
                      pl.BlockSpec((1, D_MODEL), lambda j, b, c, f: (0, 0)),
                      pl.BlockSpec(memory_space=pl.ANY)],
            out_specs=tile(D_MODEL),
            scratch_shapes=[pltpu.VMEM((2, N_EXPERTS * min(WIN, cap), D_MODEL), bf16),
                            pltpu.VMEM((N_EXPERTS, min(BIG_WIN, cap), D_MODEL), bf16),
                            pltpu.SemaphoreType.DMA((2,)),
                            pltpu.SemaphoreType.DMA((N_EXPERTS,))],
        ),
        out_shape=jax.ShapeDtypeStruct((t, D_MODEL), f32),
        compiler_params=_cparams(("arbitrary",)),
        name="combine",
    )(base, cnt, fast, rk, aff, x2, gfin, ye)


def _split_bf16(w):
    hi = w.astype(bf16)
    return jnp.concatenate([hi, (w - hi.astype(f32)).astype(bf16)], axis=1)


def _prep_layer(l, norm_mix, w_in, conv_w, a_log, dt_bias, a_norm, w_proj_a, w_proj_b, w_out,
                norm_ffn, w_router, w_gate, w_up, w_down):
    w = w_in[l]
    small0 = 4 * A_WIDTH
    small1 = small0 + 4 * A_HEADS
    qkvb1 = small1 + 3 * B_HEADS * HEAD_DIM
    qb1 = small1 + B_HEADS * HEAD_DIM
    w_main = jnp.concatenate([w[:, :small0], w[:, qkvb1:], w[:, small1:qb1] * (HEAD_DIM ** -0.5), w[:, qb1:qkvb1]],
                             axis=1).astype(bf16)
    w_small = jnp.pad(w[:, small0:small1], ((0, 0), (0, LANES - 4 * A_HEADS))).astype(bf16)
    par = jnp.zeros((8, LANES), f32)
    par = par.at[0, 2 * A_HEADS:4 * A_HEADS].set(jnp.exp(a_log[l].astype(f32)).reshape(-1))
    par = par.at[1, 2 * A_HEADS:4 * A_HEADS].set(dt_bias[l].astype(f32).reshape(-1))
    return dict(
        g_mix=norm_mix[l].reshape(1, D_MODEL), w_main=w_main, w_small=w_small, conv_w=conv_w[l], par=par,
        an=jnp.tile(a_norm[l], 2).reshape(1, LANES),
        wpa=w_proj_a[l].astype(bf16), wpb=w_proj_b[l].astype(bf16), wout=w_out[l].astype(bf16),
        g_ffn=norm_ffn[l].reshape(1, D_MODEL),
        wr=_split_bf16(jnp.pad(w_router[l], ((0, 0), (0, LANES - N_EXPERTS)))),
        wg=w_gate[l].astype(bf16), wu=w_up[l].astype(bf16), wd=w_down[l].astype(bf16),
    )


def _layer(x2, b, s, lw, gfin, final_norm):
    t = b * s
    main2, small2 = _inproj(x2, lw["g_mix"], lw["w_main"], lw["w_small"])
    main3 = main2.reshape(b, s, MAIN_WIDTH)
    qkvn, gb3, phase_views = _prep(main3, small2.reshape(b, s, LANES), lw["conv_w"], lw["par"])
    oa = _delta(qkvn, main3, gb3, lw["an"])
    obs, lses = [], []
    for g in range(len(B_GROUPS)):
        o_g, l_g = _attention(main3, g, phase_views.get(g))
        obs.append(o_g)
        lses.append(l_g)
    x2, hn, aff, afft = _merge_router(x2, oa.reshape(t, A_WIDTH), obs, lses, main2, lw["wpa"], lw["wpb"], lw["wout"],
                                      lw["g_ffn"], lw["wr"])
    cap = (CAPACITY_FACTOR * t) // N_EXPERTS
    thr, need = _threshold(afft, cap)
    pad_row = lambda a: jnp.pad(a[:, 0], (0, LANES - N_EXPERTS)).reshape(1, LANES)
    rk, tb = _select(aff, pad_row(thr), pad_row(need))
    base = tb[:, 0, :N_EXPERTS].reshape(-1)
    cnt = tb[:, 1, :N_EXPERTS].reshape(-1)
    fast = (jnp.max(tb[:, 1, :N_EXPERTS], axis=1) <= SMALL_BUCKET).astype(i32)
    xe = _dispatch(base, cnt, fast, rk, hn, cap)
    ye = _ffn(xe, lw["wg"], lw["wu"], lw["wd"], cap)
    return _combine(base, cnt, fast, rk, aff, x2, gfin, ye, cap, final_norm)


def kernel(x_prompt, x_sample, norm_mix, w_in, conv_w, a_log, dt_bias, a_norm, w_proj_a, w_proj_b, w_out,
           norm_ffn, w_router, w_gate, w_up, w_down, norm_final):
    depth = w_in.shape[0]
    layers = [_prep_layer(l, norm_mix, w_in, conv_w, a_log, dt_bias, a_norm, w_proj_a, w_proj_b, w_out,
                          norm_ffn, w_router, w_gate, w_up, w_down) for l in range(depth)]
    gfin = norm_final.reshape(1, D_MODEL)
    outs = []
    for x in (x_prompt, x_sample):
        b, s, d = x.shape
        x2 = x.reshape(b * s, d)
        for l in range(depth):
            x2 = _layer(x2, b, s, layers[l], gfin, l == depth - 1)
        outs.append(x2.reshape(b, s, d))
    return tuple(outs)
```
